```python
import jax, jax.numpy as jnp
from jax import lax
import numpy as np

D_MODEL = 1024
BATCH = 8
SEQ = 8192
DEPTH = 4
DEC_BATCH = 8
DEC_SEQ = 16
PAST_LEN = 2048

CHUNK = 64
N_MEM = 256
GMLP_GROUPS = 8
GMLP_GROUP_DIM = 64
GMLP_WIDTH = GMLP_GROUPS * GMLP_GROUP_DIM
GMLP_BLOCK = 128
FOX_HEADS = 8
FOX_HEAD_DIM = 128
FOX_WIDTH = FOX_HEADS * FOX_HEAD_DIM
Q_BLOCK = 128
MEM_HEADS = 4
MEM_HEAD_DIM = 128
MEM_WIDTH = MEM_HEADS * MEM_HEAD_DIM
N_BRANCH = 3
D_FF = 2816
N_EXPERTS = 8
TOP_K = 2
D_FF_EXPERT = 3584
N_DENSE = (DEPTH + 1) // 2
N_MOE = DEPTH // 2
EPS = 1e-6
OFF_Q = 2 * GMLP_WIDTH
OFF_K = OFF_Q + FOX_WIDTH
OFF_V = OFF_K + FOX_WIDTH
OFF_F = OFF_V + FOX_WIDTH
OFF_MQ = OFF_F + FOX_HEADS
OFF_G = OFF_MQ + MEM_WIDTH
D_IN = OFF_G + N_BRANCH * D_MODEL

kernel_name = 'hybrid_gmlp_fox_memory_stream_step'


def rmsnorm(x, g):
    xf = x.astype(jnp.float32)
    y = xf * lax.rsqrt(jnp.mean(xf * xf, axis=-1, keepdims=True) + EPS)
    return (y * g.astype(jnp.float32)).astype(x.dtype)


def layernorm(x, g, b):
    xf = x.astype(jnp.float32)
    mu = jnp.mean(xf, axis=-1, keepdims=True)
    var = jnp.mean(jnp.square(xf - mu), axis=-1, keepdims=True)
    y = (xf - mu) * lax.rsqrt(var + EPS)
    return (y * g.astype(jnp.float32) + b.astype(jnp.float32)).astype(x.dtype)


def mixer_inputs(xn, w_in, b_f, ln_g, ln_b):
    Bn, L = xn.shape[:2]
    z = xn @ w_in
    uv = jax.nn.gelu(z[..., :OFF_Q], approximate=False)
    u = uv[..., :GMLP_WIDTH]
    vn = layernorm(uv[..., GMLP_WIDTH:], ln_g, ln_b)
    q = z[..., OFF_Q:OFF_K].reshape(Bn, L, FOX_HEADS, FOX_HEAD_DIM)
    k = z[..., OFF_K:OFF_V].reshape(Bn, L, FOX_HEADS, FOX_HEAD_DIM)
    v = z[..., OFF_V:OFF_F].reshape(Bn, L, FOX_HEADS, FOX_HEAD_DIM)
    logf = jax.nn.log_sigmoid(z[..., OFF_F:OFF_MQ].astype(jnp.float32) + b_f.astype(jnp.float32))
    mq = z[..., OFF_MQ:OFF_G].reshape(Bn, L, MEM_HEADS, MEM_HEAD_DIM)
    gates = jax.nn.sigmoid(z[..., OFF_G:].reshape(Bn, L, N_BRANCH, D_MODEL))
    return u, vn, q, k, v, logf, mq, gates


def spatial_gate(u, vn, w_s, b_s, blk):
    Bn, L = u.shape[:2]
    nblk = L // blk
    W = w_s[:, :blk, :blk] * jnp.tril(jnp.ones((blk, blk), w_s.dtype))
    vb = vn.reshape(Bn, nblk, blk, GMLP_GROUPS, GMLP_GROUP_DIM)
    mixed = jnp.einsum('gts,bnsgd->bntgd', W, vb) + b_s[:, :blk].T[None, None, :, :, None]
    return u * mixed.reshape(Bn, L, GMLP_WIDTH)


def fox_prompt(q, k, v, logf):
    Bn, S_, H, Dh = q.shape
    nb = S_ // Q_BLOCK
    F = jnp.cumsum(logf.astype(jnp.float32), axis=1)
    Fk = F.transpose(0, 2, 1)[:, :, None, :]
    qb = q.reshape(Bn, nb, Q_BLOCK, H, Dh).transpose(1, 0, 2, 3, 4)
    Fqb = F.reshape(Bn, nb, Q_BLOCK, H).transpose(1, 0, 3, 2)
    kpos = jnp.arange(S_)
    scale = FOX_HEAD_DIM ** -0.5

    def block(args):
        qi, Fi, bi = args
        s = jnp.einsum('bqhd,bkhd->bhqk', qi, k).astype(jnp.float32) * scale + Fi[..., None] - Fk
        qpos = bi * Q_BLOCK + jnp.arange(Q_BLOCK)
        s = jnp.where(kpos[None, :] <= qpos[:, None], s, -jnp.inf)
        p = jax.nn.softmax(s, axis=-1).astype(v.dtype)
        return jnp.einsum('bhqk,bkhd->bqhd', p, v)

    out = lax.map(block, (qb, Fqb, jnp.arange(nb)))
    return out.transpose(1, 0, 2, 3, 4).reshape(Bn, S_, H * Dh)


def fox_sample(q, k_new, v_new, logf_new, k_cache, v_cache, logf_cache):
    Bn, T, H, Dh = q.shape
    P = k_cache.shape[1]
    k = jnp.concatenate([k_cache, k_new.astype(k_cache.dtype)], axis=1)
    v = jnp.concatenate([v_cache, v_new.astype(v_cache.dtype)], axis=1)
    F = jnp.cumsum(jnp.concatenate([logf_cache.astype(jnp.float32), logf_new.astype(jnp.float32)], axis=1), axis=1)
    Fq = F[:, P:].transpose(0, 2, 1)[..., None]
    Fk = F.transpose(0, 2, 1)[:, :, None, :]
    s = jnp.einsum('bqhd,bkhd->bhqk', q, k).astype(jnp.float32) * (FOX_HEAD_DIM ** -0.5) + Fq - Fk
    mask = jnp.arange(P + T)[None, :] <= (P + jnp.arange(T))[:, None]
    s = jnp.where(mask, s, -jnp.inf)
    p = jax.nn.softmax(s, axis=-1).astype(v.dtype)
    return jnp.einsum('bhqk,bkhd->bqhd', p, v).reshape(Bn, T, H * Dh)


def memory_kv(mem, g_mem, w_mem_kv):
    Bn = mem.shape[0]
    kv = rmsnorm(mem, g_mem) @ w_mem_kv
    mk = kv[..., :MEM_WIDTH].reshape(Bn, N_MEM, MEM_HEADS, MEM_HEAD_DIM)
    mv = kv[..., MEM_WIDTH:].reshape(Bn, N_MEM, MEM_HEADS, MEM_HEAD_DIM)
    return mk, mv


def memory_attend(mq, mk, mv):
    Bn, L = mq.shape[:2]
    s = jnp.einsum('bqhd,bkhd->bhqk', mq, mk.astype(mq.dtype)).astype(jnp.float32) * (MEM_HEAD_DIM ** -0.5)
    p = jax.nn.softmax(s, axis=-1).astype(mq.dtype)
    return jnp.einsum('bhqk,bkhd->bqhd', p, mv.astype(mq.dtype)).reshape(Bn, L, MEM_WIDTH)


def merge_branches(a, b, m, gates, w_pa, w_pb, w_pm, w_o):
    y = gates[:, :, 0] * (a @ w_pa) + gates[:, :, 1] * (b @ w_pb) + gates[:, :, 2] * (m @ w_pm)
    return y @ w_o


def swiglu(h, w1, w3, w2):
    return (jax.nn.silu(h @ w1) * (h @ w3)) @ w2


def moe(h, w_router, e_w1, e_w3, e_w2):
    logits = (h @ w_router).astype(jnp.float32)
    top_v, top_i = lax.top_k(logits, TOP_K)
    top_p = jax.nn.softmax(top_v, axis=-1)
    combine = jnp.sum(jax.nn.one_hot(top_i, N_EXPERTS, dtype=jnp.float32) * top_p[..., None], axis=-2)
    out = jnp.zeros_like(h)
    for e in range(N_EXPERTS):
        out = out + combine[..., e:e + 1].astype(h.dtype) * swiglu(h, e_w1[e], e_w3[e], e_w2[e])
    return out


def channel_mix(h, i, w1, w3, w2, w_router, e_w1, e_w3, e_w2):
    j = i // 2
    if i % 2 == 0:
        return swiglu(h, w1[j], w3[j], w2[j])
    return moe(h, w_router[j], e_w1[j], e_w3[j], e_w2[j])


def setup_inputs(seed: int = 0) -> dict:
    key = jax.random.key(seed)
    ks = iter(jax.random.split(key, 40))

    def nrm(shape, scale):
        return jax.random.normal(next(ks), shape, jnp.float32) * scale

    def gain(shape):
        return 1.0 + 0.05 * jax.random.normal(next(ks), shape, jnp.float32)

    D = D_MODEL
    return {
        'x_prompt': nrm((BATCH, SEQ, D), 1.0),
        'x_sample': nrm((DEC_BATCH, DEC_SEQ, D), 1.0),
        'cache_fox_k': nrm((DEPTH, DEC_BATCH, PAST_LEN, FOX_HEADS, FOX_HEAD_DIM), 1.0),
        'cache_fox_v': nrm((DEPTH, DEC_BATCH, PAST_LEN, FOX_HEADS, FOX_HEAD_DIM), 1.0),
        'cache_fox_logf': jax.nn.log_sigmoid(3.0 + nrm((DEPTH, DEC_BATCH, PAST_LEN, FOX_HEADS), 1.0)),
        'cache_mem_k': nrm((DEPTH, DEC_BATCH, N_MEM, MEM_HEADS, MEM_HEAD_DIM), 1.0),
        'cache_mem_v': nrm((DEPTH, DEC_BATCH, N_MEM, MEM_HEADS, MEM_HEAD_DIM), 1.0),
        'mem_prompt': nrm((BATCH, N_MEM, D), 1.0),
        'g_mix': gain((DEPTH, D)),
        'w_in': nrm((DEPTH, D, D_IN), D ** -0.5),
        'b_f': 3.0 + nrm((DEPTH, FOX_HEADS), 0.1),
        'ln_v_g': gain((DEPTH, GMLP_WIDTH)),
        'ln_v_b': nrm((DEPTH, GMLP_WIDTH), 0.02),
        'w_s': nrm((DEPTH, GMLP_GROUPS, GMLP_BLOCK, GMLP_BLOCK), GMLP_BLOCK ** -0.5),
        'b_s': 1.0 + nrm((DEPTH, GMLP_GROUPS, GMLP_BLOCK), 0.1),
        'g_mem': gain((DEPTH, D)),
        'w_mem_kv': nrm((DEPTH, D, 2 * MEM_WIDTH), D ** -0.5),
        'w_pa': nrm((DEPTH, GMLP_WIDTH, D), GMLP_WIDTH ** -0.5),
        'w_pb': nrm((DEPTH, FOX_WIDTH, D), FOX_WIDTH ** -0.5),
        'w_pm': nrm((DEPTH, MEM_WIDTH, D), MEM_WIDTH ** -0.5),
        'w_o': nrm((DEPTH, D, D), D ** -0.5),
        'g_ffn': gain((DEPTH, D)),
        'w1': nrm((N_DENSE, D, D_FF), D ** -0.5),
        'w3': nrm((N_DENSE, D, D_FF), D ** -0.5),
        'w2': nrm((N_DENSE, D_FF, D), D_FF ** -0.5),
        'w_router': nrm((N_MOE, D, N_EXPERTS), D ** -0.5),
        'e_w1': nrm((N_MOE, N_EXPERTS, D, D_FF_EXPERT), D ** -0.5),
        'e_w3': nrm((N_MOE, N_EXPERTS, D, D_FF_EXPERT), D ** -0.5),
        'e_w2': nrm((N_MOE, N_EXPERTS, D_FF_EXPERT, D), D_FF_EXPERT ** -0.5),
        'g_final': gain((D,)),
    }


def reference(x_prompt, x_sample, cache_fox_k, cache_fox_v, cache_fox_logf, cache_mem_k, cache_mem_v,
              mem_prompt, g_mix, w_in, b_f, ln_v_g, ln_v_b, w_s, b_s, g_mem, w_mem_kv,
              w_pa, w_pb, w_pm, w_o, g_ffn, w1, w3, w2, w_router, e_w1, e_w3, e_w2, g_final):
    T = x_sample.shape[1]
    assert T <= CHUNK <= GMLP_BLOCK
    hp, hs = x_prompt, x_sample
    fk_p, fv_p, fl_p, mk_p, mv_p = [], [], [], [], []
    fk_s, fv_s, fl_s, gv_s = [], [], [], []
    for i in range(DEPTH):
        u, vn, q, k, v, logf, mq, gates = mixer_inputs(rmsnorm(hp, g_mix[i]), w_in[i], b_f[i], ln_v_g[i], ln_v_b[i])
        mk, mv = memory_kv(mem_prompt, g_mem[i], w_mem_kv[i])
        a = spatial_gate(u, vn, w_s[i], b_s[i], GMLP_BLOCK)
        b = fox_prompt(q, k, v, logf)
        m = memory_attend(mq, mk, mv)
        hp = hp + merge_branches(a, b, m, gates, w_pa[i], w_pb[i], w_pm[i], w_o[i])
        hp = hp + channel_mix(rmsnorm(hp, g_ffn[i]), i, w1, w3, w2, w_router, e_w1, e_w3, e_w2)
        fk_p.append(k)
        fv_p.append(v)
        fl_p.append(logf)
        mk_p.append(mk)
        mv_p.append(mv)
        u, vn, q, k, v, logf, mq, gates = mixer_inputs(rmsnorm(hs, g_mix[i]), w_in[i], b_f[i], ln_v_g[i], ln_v_b[i])
        a = spatial_gate(u, vn, w_s[i], b_s[i], T)
        b = fox_sample(q, k, v, logf, cache_fox_k[i], cache_fox_v[i], cache_fox_logf[i])
        m = memory_attend(mq, cache_mem_k[i], cache_mem_v[i])
        hs = hs + merge_branches(a, b, m, gates, w_pa[i], w_pb[i], w_pm[i], w_o[i])
        hs = hs + channel_mix(rmsnorm(hs, g_ffn[i]), i, w1, w3, w2, w_router, e_w1, e_w3, e_w2)
        fk_s.append(k)
        fv_s.append(v)
        fl_s.append(logf)
        gv_s.append(vn)
    y_prompt = rmsnorm(hp, g_final)
    y_sample = rmsnorm(hs, g_final)
    return (y_prompt, y_sample,
            jnp.stack(fk_p), jnp.stack(fv_p), jnp.stack(fl_p), jnp.stack(mk_p), jnp.stack(mv_p),
            jnp.stack(fk_s), jnp.stack(fv_s), jnp.stack(fl_s), jnp.stack(gv_s))
```

```python
import functools

import numpy as np
import jax
import jax.numpy as jnp
from jax import lax
from jax.experimental import pallas as pl
from jax.experimental.pallas import tpu as pltpu

F32 = jnp.float32
BF16 = jnp.bfloat16

EPS = 1e-6
GMLP_GROUPS = 8
GMLP_GROUP_DIM = 64
GMLP_WIDTH = GMLP_GROUPS * GMLP_GROUP_DIM
GMLP_BLOCK = 128
FOX_HEADS = 8
FOX_HEAD_DIM = 128
FOX_WIDTH = FOX_HEADS * FOX_HEAD_DIM
MEM_HEADS = 4
MEM_HEAD_DIM = 128
MEM_WIDTH = MEM_HEADS * MEM_HEAD_DIM
N_BRANCH = 3
TOP_K = 2
LANES = 128
VMEM_LIMIT = 56 * 1024 * 1024

_NT = (((1,), (1,)), ((), ()))


def _cparams(*sem):
    return pltpu.CompilerParams(dimension_semantics=sem, vmem_limit_bytes=VMEM_LIMIT)


def _rms(x, g):
    return x * lax.rsqrt(jnp.mean(x * x, axis=-1, keepdims=True) + EPS) * g


def _sigmoid(x):
    return 1.0 / (1.0 + jnp.exp(-x))


def _dot(a, b):
    return jnp.dot(a, b, preferred_element_type=F32)


def _dot_nt(a, b):
    return lax.dot_general(a, b, _NT, preferred_element_type=F32)


def _uv_kernel(h_ref, g_ref, w_ref, lng_ref, lnb_ref, ws_ref, bs_ref, a_ref, vn_ref):
    tm = h_ref.shape[0]
    xn = _rms(h_ref[...], g_ref[...]).astype(BF16)
    z = _dot(xn, w_ref[...])
    uv = 0.5 * z * (1.0 + lax.erf(z * np.float32(2.0 ** -0.5)))
    u = uv[:, :GMLP_WIDTH]
    v = uv[:, GMLP_WIDTH:]
    mu = jnp.mean(v, axis=-1, keepdims=True)
    vc = v - mu
    var = jnp.mean(vc * vc, axis=-1, keepdims=True)
    vn = vc * lax.rsqrt(var + EPS) * lng_ref[...] + lnb_ref[...]
    vn_ref[...] = vn
    lane = lax.broadcasted_iota(jnp.int32, (GMLP_BLOCK, LANES), 1)
    low = lane < GMLP_GROUP_DIM
    for r in range(tm // GMLP_BLOCK):
        rows = slice(r * GMLP_BLOCK, (r + 1) * GMLP_BLOCK)
        for p in range(GMLP_WIDTH // LANES):
            cols = slice(p * LANES, (p + 1) * LANES)
            vs = vn[rows, cols].astype(BF16)
            zero = jnp.zeros_like(vs)
            rhs = jnp.concatenate([jnp.where(low, vs, zero), jnp.where(low, zero, vs)], axis=0)
            mixed = _dot(ws_ref[p], rhs) + bs_ref[p]
            a_ref[rows, cols] = (u[rows, cols] * mixed).astype(a_ref.dtype)


def _uv_call(h, g, w_uv, ln_g, ln_b, ws_pairs, bs_pairs, tm):
    n, d = h.shape
    const = lambda *shape: pl.BlockSpec(shape, lambda t: (0,) * len(shape))
    return pl.pallas_call(
        _uv_kernel,
        grid=(n // tm,),
        in_specs=[
            pl.BlockSpec((tm, d), lambda t: (t, 0)),
            const(1, d),
            const(d, 2 * GMLP_WIDTH),
            const(1, GMLP_WIDTH),
            const(1, GMLP_WIDTH),
            const(*ws_pairs.shape),
            const(*bs_pairs.shape),
        ],
        out_specs=[
            pl.BlockSpec((tm, GMLP_WIDTH), lambda t: (t, 0)),
            pl.BlockSpec((tm, GMLP_WIDTH), lambda t: (t, 0)),
        ],
        out_shape=[
            jax.ShapeDtypeStruct((n, GMLP_WIDTH), BF16),
            jax.ShapeDtypeStruct((n, GMLP_WIDTH), F32),
        ],
        compiler_params=_cparams("parallel"),
        name="inproj_gmlp",
    )(h, g, w_uv, ln_g, ln_b, ws_pairs, bs_pairs)


def _qkv_kernel(h_ref, g_ref, w_ref, q_ref, k_ref, v_ref, kb_ref, vb_ref):
    xn = _rms(h_ref[...], g_ref[...]).astype(BF16)
    scale = np.float32(FOX_HEAD_DIM ** -0.5)
    q_ref[...] = (_dot(xn, w_ref[:, :FOX_WIDTH]) * scale).astype(BF16)
    k = _dot(xn, w_ref[:, FOX_WIDTH:2 * FOX_WIDTH])
    k_ref[...] = k
    kb_ref[...] = k.astype(BF16)
    v = _dot(xn, w_ref[:, 2 * FOX_WIDTH:])
    v_ref[...] = v
    vb_ref[...] = v.astype(BF16)


def _qkv_call(h, g, w_qkv, tm):
    n, d = h.shape
    row = lambda width: pl.BlockSpec((tm, width), lambda t: (t, 0))
    return pl.pallas_call(
        _qkv_kernel,
        grid=(n // tm,),
        in_specs=[row(d), pl.BlockSpec((1, d), lambda t: (0, 0)),
                  pl.BlockSpec((d, 3 * FOX_WIDTH), lambda t: (0, 0))],
        out_specs=[row(FOX_WIDTH)] * 5,
        out_shape=[
            jax.ShapeDtypeStruct((n, FOX_WIDTH), BF16),
            jax.ShapeDtypeStruct((n, FOX_WIDTH), F32),
            jax.ShapeDtypeStruct((n, FOX_WIDTH), F32),
            jax.ShapeDtypeStruct((n, FOX_WIDTH), BF16),
            jax.ShapeDtypeStruct((n, FOX_WIDTH), BF16),
        ],
        compiler_params=_cparams("parallel"),
        name="inproj_qkv",
    )(h, g, w_qkv)


def _fmg_kernel(h_ref, g_ref, wft_ref, bf_ref, wmq_ref, wg_ref, mk_ref, mv_ref,
                logf_ref, m_ref, gates_ref):
    xn = _rms(h_ref[...], g_ref[...]).astype(BF16)
    zf = _dot_nt(wft_ref[...], xn)[:FOX_HEADS] + bf_ref[...]
    logf_ref[0] = jnp.minimum(zf, 0.0) - jnp.log1p(jnp.exp(-jnp.abs(zf)))
    mq = (_dot(xn, wmq_ref[...]) * np.float32(MEM_HEAD_DIM ** -0.5)).astype(BF16)
    for hh in range(MEM_HEADS):
        cols = slice(hh * MEM_HEAD_DIM, (hh + 1) * MEM_HEAD_DIM)
        s = _dot_nt(mq[:, cols], mk_ref[:, cols].astype(BF16))
        p = jnp.exp(s - jnp.max(s, axis=-1, keepdims=True))
        p = (p / jnp.sum(p, axis=-1, keepdims=True)).astype(BF16)
        m_ref[:, cols] = _dot(p, mv_ref[:, cols].astype(BF16)).astype(m_ref.dtype)
    gates_ref[...] = _sigmoid(_dot(xn, wg_ref[...])).astype(gates_ref.dtype)


def _fmg_call(h, g, w_ft, b_f, w_mq, w_g, mk, mv, batch, tm):
    n, d = h.shape
    seq = n // batch
    tpb = seq // tm
    n_mem = mk.shape[0] // batch
    const = lambda *shape: pl.BlockSpec(shape, lambda t: (0,) * len(shape))
    return pl.pallas_call(
        _fmg_kernel,
        grid=(n // tm,),
        in_specs=[
            pl.BlockSpec((tm, d), lambda t: (t, 0)),
            const(1, d),
            const(*w_ft.shape),
            const(FOX_HEADS, 1),
            const(d, MEM_WIDTH),
            const(d, N_BRANCH * d),
            pl.BlockSpec((n_mem, MEM_WIDTH), lambda t: (t // tpb, 0)),
            pl.BlockSpec((n_mem, MEM_WIDTH), lambda t: (t // tpb, 0)),
        ],
        out_specs=[
            pl.BlockSpec((1, FOX_HEADS, tm), lambda t: (t // tpb, 0, t % tpb)),
            pl.BlockSpec((tm, MEM_WIDTH), lambda t: (t, 0)),
            pl.BlockSpec((tm, N_BRANCH * d), lambda t: (t, 0)),
        ],
        out_shape=[
            jax.ShapeDtypeStruct((batch, FOX_HEADS, seq), F32),
            jax.ShapeDtypeStruct((n, MEM_WIDTH), BF16),
            jax.ShapeDtypeStruct((n, N_BRANCH * d), BF16),
        ],
        compiler_params=_cparams("parallel"),
        name="inproj_logf_mem_gates",
    )(h, g, w_ft, b_f, w_mq, w_g, mk, mv)


def _memkv_kernel(x_ref, g_ref, w_ref, mk_ref, mv_ref):
    xn = _rms(x_ref[...], g_ref[...]).astype(BF16)
    mk_ref[...] = _dot(xn, w_ref[:, :MEM_WIDTH])
    mv_ref[...] = _dot(xn, w_ref[:, MEM_WIDTH:])


def _memkv_call(mem, g, w_kv, tm):
    n, d = mem.shape
    return pl.pallas_call(
        _memkv_kernel,
        grid=(n // tm,),
        in_specs=[pl.BlockSpec((tm, d), lambda t: (t, 0)),
                  pl.BlockSpec((1, d), lambda t: (0, 0)),
                  pl.BlockSpec((d, 2 * MEM_WIDTH), lambda t: (0, 0))],
        out_specs=[pl.BlockSpec((tm, MEM_WIDTH), lambda t: (t, 0))] * 2,
        out_shape=[jax.ShapeDtypeStruct((n, MEM_WIDTH), F32)] * 2,
        compiler_params=_cparams("parallel"),
        name="memory_kv",
    )(mem, g, w_kv)


def _neg_cumsum_kernel(x_ref, o_ref):
    x = x_ref[...]
    width = x.shape[-1]
    lane = lax.broadcasted_iota(jnp.int32, x.shape, 1)
    d = 1
    while d < width:
        x = x + jnp.where(lane >= d, pltpu.roll(x, d, 1), 0.0)
        d *= 2
    o_ref[...] = -x


def _neg_cumsum_call(x):
    return pl.pallas_call(
        _neg_cumsum_kernel,
        out_shape=jax.ShapeDtypeStruct(x.shape, F32),
        compiler_params=pltpu.CompilerParams(vmem_limit_bytes=VMEM_LIMIT),
        name="logf_cumsum",
    )(x)


def _fox_kernel(qi_ref, kj_ref, flag_ref, q_ref, k_ref, v_ref, nf_ref, o_ref, m_sc, l_sc, acc_sc,
                *, tq, tk):
    step = pl.program_id(2)
    flags = flag_ref[step]

    @pl.when(flags & 1 != 0)
    def _():
        m_sc[...] = jnp.full_like(m_sc, -jnp.inf)
        l_sc[...] = jnp.zeros_like(l_sc)
        acc_sc[...] = jnp.zeros_like(acc_sc)

    def update(masked):
        s = _dot_nt(q_ref[...], k_ref[...]) + nf_ref[0]
        if masked:
            row = qi_ref[step] * tq + lax.broadcasted_iota(jnp.int32, (tq, tk), 0)
            col = kj_ref[step] * tk + lax.broadcasted_iota(jnp.int32, (tq, tk), 1)
            s = jnp.where(col <= row, s, -jnp.inf)
        m_prev = m_sc[...]
        m_new = jnp.maximum(m_prev, jnp.max(s, axis=-1, keepdims=True))
        alpha = jnp.exp(m_prev - m_new)
        p = jnp.exp(s - m_new)
        l_sc[...] = alpha * l_sc[...] + jnp.sum(p, axis=-1, keepdims=True)
        acc_sc[...] = alpha * acc_sc[...] + _dot(p.astype(BF16), v_ref[...])
        m_sc[...] = m_new

    @pl.when(flags & 4 != 0)
    def _():
        update(True)

    @pl.when(flags & 4 == 0)
    def _():
        update(False)

    @pl.when(flags & 2 != 0)
    def _():
        o_ref[...] = (acc_sc[...] / l_sc[...]).astype(o_ref.dtype)


def _fox_tables(seq, tq, tk):
    qi, kj, flags = [], [], []
    for i in range(seq // tq):
        last_j = (i * tq + tq - 1) // tk
        for j in range(last_j + 1):
            masked = (j * tk + tk - 1) > (i * tq)
            qi.append(i)
            kj.append(j)
            flags.append((1 if j == 0 else 0) | (2 if j == last_j else 0) | (4 if masked else 0))
    return (np.asarray(qi, np.int32), np.asarray(kj, np.int32), np.asarray(flags, np.int32))


def _fox_call(q, k, v, neg_f, batch, tq, tk):
    n, width = q.shape
    seq = n // batch
    heads = width // FOX_HEAD_DIM
    qi, kj, flags = _fox_tables(seq, tq, tk)
    nq, nk = seq // tq, seq // tk
    nf = neg_f.reshape(batch * heads, 1, seq)
    grid_spec = pltpu.PrefetchScalarGridSpec(
        num_scalar_prefetch=3,
        grid=(batch, heads, len(qi)),
        in_specs=[
            pl.BlockSpec((tq, FOX_HEAD_DIM), lambda b, h, s, qi, kj, fl: (b * nq + qi[s], h)),
            pl.BlockSpec((tk, FOX_HEAD_DIM), lambda b, h, s, qi, kj, fl: (b * nk + kj[s], h)),
            pl.BlockSpec((tk, FOX_HEAD_DIM), lambda b, h, s, qi, kj, fl: (b * nk + kj[s], h)),
            pl.BlockSpec((1, 1, tk), lambda b, h, s, qi, kj, fl: (b * heads + h, 0, kj[s])),
        ],
        out_specs=pl.BlockSpec((tq, FOX_HEAD_DIM), lambda b, h, s, qi, kj, fl: (b * nq + qi[s], h)),
        scratch_shapes=[
            pltpu.VMEM((tq, 1), F32),
            pltpu.VMEM((tq, 1), F32),
            pltpu.VMEM((tq, FOX_HEAD_DIM), F32),
        ],
    )
    return pl.pallas_call(
        functools.partial(_fox_kernel, tq=tq, tk=tk),
        grid_spec=grid_spec,
        out_shape=jax.ShapeDtypeStruct((n, width), BF16),
        compiler_params=_cparams("parallel", "parallel", "arbitrary"),
        name="fox_attention",
    )(jnp.asarray(qi), jnp.asarray(kj), jnp.asarray(flags), q, k, v, nf)


def _fox_sample_kernel(q_ref, kc_ref, vc_ref, kn_ref, vn_ref, nf_ref, o_ref):
    t = q_ref.shape[0]
    past = kc_ref.shape[0]
    q = q_ref[...]
    nf = nf_ref[0]
    pad = jnp.zeros((LANES - t, FOX_HEAD_DIM), BF16)
    kn = jnp.concatenate([kn_ref[...].astype(BF16), pad], axis=0)
    vn = jnp.concatenate([vn_ref[...].astype(BF16), pad], axis=0)
    s_c = _dot_nt(q, kc_ref[...].astype(BF16)) + nf[:, :past]
    s_n = _dot_nt(q, kn) + nf[:, past:]
    row = lax.broadcasted_iota(jnp.int32, (t, LANES), 0)
    col = lax.broadcasted_iota(jnp.int32, (t, LANES), 1)
    s_n = jnp.where(col <= row, s_n, -jnp.inf)
    m = jnp.maximum(jnp.max(s_c, axis=-1, keepdims=True), jnp.max(s_n, axis=-1, keepdims=True))
    p_c = jnp.exp(s_c - m)
    p_n = jnp.exp(s_n - m)
    l = jnp.sum(p_c, axis=-1, keepdims=True) + jnp.sum(p_n, axis=-1, keepdims=True)
    acc = _dot(p_c.astype(BF16), vc_ref[...].astype(BF16)) + _dot(p_n.astype(BF16), vn)
    o_ref[...] = (acc / l).astype(o_ref.dtype)


def _fox_sample_call(q, k_cache, v_cache, k_new, v_new, neg_f, batch):
    n, width = q.shape
    t = n // batch
    heads = width // FOX_HEAD_DIM
    past = k_cache.shape[0] // batch
    nf = neg_f.reshape(batch * heads, 1, past + LANES)
    new = pl.BlockSpec((t, FOX_HEAD_DIM), lambda b, h: (b, h))
    old = pl.BlockSpec((past, FOX_HEAD_DIM), lambda b, h: (b, h))
    return pl.pallas_call(
        _fox_sample_kernel,
        grid=(batch, heads),
        in_specs=[new, old, old, new, new,
                  pl.BlockSpec((1, 1, past + LANES), lambda b, h: (b * heads + h, 0, 0))],
        out_specs=new,
        out_shape=jax.ShapeDtypeStruct((n, width), BF16),
        compiler_params=_cparams("parallel", "parallel"),
        name="fox_attention_sample",
    )(q, k_cache, v_cache, k_new, v_new, nf)


def _merge_kernel(h_ref, a_ref, b_ref, m_ref, gates_ref, wpa_ref, wpb_ref, wpm_ref, wo_ref, o_ref):
    d = h_ref.shape[1]
    y = gates_ref[:, :d].astype(F32) * _dot(a_ref[...], wpa_ref[...])
    y = y + gates_ref[:, d:2 * d].astype(F32) * _dot(b_ref[...], wpb_ref[...])
    y = y + gates_ref[:, 2 * d:].astype(F32) * _dot(m_ref[...], wpm_ref[...])
    o_ref[...] = h_ref[...] + _dot(y.astype(BF16), wo_ref[...])


def _merge_call(h, a, b, m, gates, w_pa, w_pb, w_pm, w_o, tm):
    n, d = h.shape
    row = lambda width: pl.BlockSpec((tm, width), lambda t: (t, 0))
    const = lambda w: pl.BlockSpec(w.shape, lambda t: (0, 0))
    return pl.pallas_call(
        _merge_kernel,
        grid=(n // tm,),
        in_specs=[row(d), row(GMLP_WIDTH), row(FOX_WIDTH), row(MEM_WIDTH), row(N_BRANCH * d),
                  const(w_pa), const(w_pb), const(w_pm), const(w_o)],
        out_specs=row(d),
        out_shape=jax.ShapeDtypeStruct((n, d), F32),
        compiler_params=_cparams("parallel"),
        name="merge_branches",
    )(h, a, b, m, gates, w_pa, w_pb, w_pm, w_o)


def _top2_combine(logits, n_exp):
    lane = lax.broadcasted_iota(jnp.int32, logits.shape, 1)
    neg = jnp.float32(-jnp.inf)
    logits = jnp.where(lane < n_exp, logits, neg)
    v1 = jnp.max(logits, axis=-1, keepdims=True)
    i1 = jnp.min(jnp.where(logits == v1, lane, LANES), axis=-1, keepdims=True)
    rest = jnp.where(lane == i1, neg, logits)
    v2 = jnp.max(rest, axis=-1, keepdims=True)
    i2 = jnp.min(jnp.where(rest == v2, lane, LANES), axis=-1, keepdims=True)
    e2 = jnp.exp(v2 - v1)
    p1 = 1.0 / (1.0 + e2)
    p2 = e2 / (1.0 + e2)
    return jnp.where(lane == i1, p1, 0.0) + jnp.where(lane == i2, p2, 0.0)


def _mixer_kernel(h_ref, g_ref, *refs, routed, final_norm):
    refs = list(refs)
    wr_ref = refs.pop(0) if routed else None
    gfin_ref = refs.pop(0) if final_norm else None
    w1_ref, w3_ref, w2_ref, o_ref, xn_sc, acc_sc = refs[:6]
    comb_sc = refs[6] if routed else None
    e = pl.program_id(1)
    c = pl.program_id(2)
    first = jnp.logical_and(e == 0, c == 0)
    last = jnp.logical_and(e == pl.num_programs(1) - 1, c == pl.num_programs(2) - 1)

    @pl.when(first)
    def _():
        xn = _rms(h_ref[...], g_ref[...])
        xn_sc[...] = xn.astype(BF16)
        acc_sc[...] = jnp.zeros_like(acc_sc)
        if routed:
            logits = jnp.dot(xn, wr_ref[...], preferred_element_type=F32,
                             precision=lax.Precision.HIGHEST)
            comb_sc[...] = _top2_combine(logits, pl.num_programs(1))

    xb = xn_sc[...]
    g1 = _dot(xb, w1_ref[0])
    g3 = _dot(xb, w3_ref[0])
    mid = (g1 * _sigmoid(g1) * g3).astype(BF16)
    y = _dot(mid, w2_ref[0])
    if routed:
        lane = lax.broadcasted_iota(jnp.int32, comb_sc.shape, 1)
        y = y * jnp.sum(jnp.where(lane == e, comb_sc[...], 0.0), axis=-1, keepdims=True)
    acc_sc[...] += y

    @pl.when(last)
    def _():
        out = h_ref[...] + acc_sc[...]
        if final_norm:
            out = _rms(out, gfin_ref[...])
        o_ref[...] = out


def _mixer_call(h, g, w1, w3, w2, tm, tf, w_router=None, g_final=None):
    n, d = h.shape
    n_exp, _, ff = w1.shape
    routed = w_router is not None
    final_norm = g_final is not None
    row = pl.BlockSpec((tm, d), lambda t, e, c: (t, 0))
    vec = pl.BlockSpec((1, d), lambda t, e, c: (0, 0))
    args, in_specs = [h, g], [row, vec]
    if routed:
        args.append(w_router)
        in_specs.append(pl.BlockSpec(w_router.shape, lambda t, e, c: (0, 0)))
    if final_norm:
        args.append(g_final)
        in_specs.append(vec)
    args += [w1, w3, w2]
    in_specs += [
        pl.BlockSpec((1, d, tf), lambda t, e, c: (e, 0, c)),
        pl.BlockSpec((1, d, tf), lambda t, e, c: (e, 0, c)),
        pl.BlockSpec((1, tf, d), lambda t, e, c: (e, c, 0)),
    ]
    scratch = [pltpu.VMEM((tm, d), BF16), pltpu.VMEM((tm, d), F32)]
    if routed:
        scratch.append(pltpu.VMEM((tm, LANES), F32))
    return pl.pallas_call(
        functools.partial(_mixer_kernel, routed=routed, final_norm=final_norm),
        grid=(n // tm, n_exp, ff // tf),
        in_specs=in_specs,
        out_specs=row,
        out_shape=jax.ShapeDtypeStruct((n, d), F32),
        scratch_shapes=scratch,
        compiler_params=_cparams("parallel", "arbitrary", "arbitrary"),
        name="channel_mixer",
    )(*args)


def _largest_tile(n, cap, quantum):
    best = quantum
    t = quantum
    while t <= min(n, cap):
        if n % t == 0:
            best = t
        t += quantum
    return best


def _spatial_operands(w_s, b_s, blk):
    reps = GMLP_BLOCK // blk
    w = w_s[:, :blk, :blk] * jnp.tril(jnp.ones((blk, blk), w_s.dtype))
    eye = jnp.eye(reps, dtype=w_s.dtype)
    w = jnp.einsum("ab,gts->gatbs", eye, w).reshape(GMLP_GROUPS, GMLP_BLOCK, GMLP_BLOCK)
    b = jnp.tile(b_s[:, :blk], (1, reps))
    ws_pairs = jnp.concatenate([w[0::2], w[1::2]], axis=2).astype(BF16)
    bs_pairs = jnp.concatenate(
        [jnp.broadcast_to(b[0::2, :, None], (GMLP_GROUPS // 2, GMLP_BLOCK, GMLP_GROUP_DIM)),
         jnp.broadcast_to(b[1::2, :, None], (GMLP_GROUPS // 2, GMLP_BLOCK, GMLP_GROUP_DIM))],
        axis=2).astype(F32)
    return ws_pairs, bs_pairs


def kernel(x_prompt, x_sample, cache_fox_k, cache_fox_v, cache_fox_logf, cache_mem_k, cache_mem_v, mem_prompt, g_mix, w_in, b_f, ln_v_g, ln_v_b, w_s, b_s, g_mem, w_mem_kv, w_pa, w_pb, w_pm, w_o, g_ffn, w1, w3, w2, w_router, e_w1, e_w3, e_w2, g_final):
    depth = w_in.shape[0]
    bp, sp, d = x_prompt.shape
    bs, ts, _ = x_sample.shape
    past = cache_fox_k.shape[2]
    n_mem = mem_prompt.shape[1]
    n_exp = e_w1.shape[1]
    assert sp % GMLP_BLOCK == 0 and GMLP_BLOCK % ts == 0 and (bs * ts) % GMLP_BLOCK == 0

    off_q = 2 * GMLP_WIDTH
    off_f = off_q + 3 * FOX_WIDTH
    off_mq = off_f + FOX_HEADS
    off_g = off_mq + MEM_WIDTH

    hp = x_prompt.reshape(bp * sp, d)
    hs = x_sample.reshape(bs * ts, d)
    mem = mem_prompt.reshape(bp * n_mem, d)

    tm_p = _largest_tile(sp, 512, GMLP_BLOCK)
    tm_s = GMLP_BLOCK
    tq = _largest_tile(sp, 1024, LANES)
    tk = _largest_tile(sp, 512, LANES)
    tmix_p = _largest_tile(bp * sp, 1024, LANES)
    tmix_s = bs * ts

    outs = {k: [] for k in ("fk_p", "fv_p", "fl_p", "mk_p", "mv_p", "fk_s", "fv_s", "fl_s", "gv_s")}
    for i in range(depth):
        g = g_mix[i].reshape(1, d)
        wi = w_in[i]
        w_uv = wi[:, :off_q].astype(BF16)
        w_qkv = wi[:, off_q:off_f].astype(BF16)
        w_ft = jnp.zeros((16, d), BF16).at[:FOX_HEADS].set(wi[:, off_f:off_mq].T.astype(BF16))
        w_mq = wi[:, off_mq:off_g].astype(BF16)
        w_g = wi[:, off_g:].astype(BF16)
        bfi = b_f[i].reshape(FOX_HEADS, 1)
        lng = ln_v_g[i].reshape(1, GMLP_WIDTH)
        lnb = ln_v_b[i].reshape(1, GMLP_WIDTH)
        wpa, wpb, wpm, wo = (w[i].astype(BF16) for w in (w_pa, w_pb, w_pm, w_o))
        gf = g_ffn[i].reshape(1, d)
        j = i // 2
        if i % 2 == 0:
            mw1, mw3, mw2 = (w[j][None].astype(BF16) for w in (w1, w3, w2))
            wr = None
        else:
            mw1, mw3, mw2 = (w[j].astype(BF16) for w in (e_w1, e_w3, e_w2))
            wr = jnp.zeros((d, LANES), F32).at[:, :n_exp].set(w_router[j])
        ff = mw1.shape[2]
        tf = _largest_tile(ff, 1408, LANES)
        gfin = g_final.reshape(1, d) if i == depth - 1 else None

        ws_pairs, bs_pairs = _spatial_operands(w_s[i], b_s[i], GMLP_BLOCK)
        mk, mv = _memkv_call(mem, g_mem[i].reshape(1, d), w_mem_kv[i].astype(BF16), n_mem)
        a, _ = _uv_call(hp, g, w_uv, lng, lnb, ws_pairs, bs_pairs, tm_p)
        q, k, v, kb, vb = _qkv_call(hp, g, w_qkv, tm_p)
        logf_t, m, gates = _fmg_call(hp, g, w_ft, bfi, w_mq, w_g, mk, mv, bp, tm_p)
        neg_f = _neg_cumsum_call(logf_t.reshape(bp * FOX_HEADS, sp))
        b = _fox_call(q, kb, vb, neg_f, bp, tq, tk)
        hp = _merge_call(hp, a, b, m, gates, wpa, wpb, wpm, wo, tm_p)
        hp = _mixer_call(hp, gf, mw1, mw3, mw2, tmix_p, tf, wr, gfin)
        outs["fk_p"].append(k.reshape(bp, sp, FOX_HEADS, FOX_HEAD_DIM))
        outs["fv_p"].append(v.reshape(bp, sp, FOX_HEADS, FOX_HEAD_DIM))
        outs["fl_p"].append(logf_t.transpose(0, 2, 1))
        outs["mk_p"].append(mk.reshape(bp, n_mem, MEM_HEADS, MEM_HEAD_DIM))
        outs["mv_p"].append(mv.reshape(bp, n_mem, MEM_HEADS, MEM_HEAD_DIM))

        ws_pairs, bs_pairs = _spatial_operands(w_s[i], b_s[i], ts)
        a, vn = _uv_call(hs, g, w_uv, lng, lnb, ws_pairs, bs_pairs, tm_s)
        q, k, v, _, _ = _qkv_call(hs, g, w_qkv, bs * ts)
        logf_t, m, gates = _fmg_call(hs, g, w_ft, bfi, w_mq, w_g,
                                     cache_mem_k[i].reshape(bs * n_mem, MEM_WIDTH),
                                     cache_mem_v[i].reshape(bs * n_mem, MEM_WIDTH), bs, ts)
        logf_all = jnp.concatenate(
            [cache_fox_logf[i].transpose(0, 2, 1), logf_t,
             jnp.zeros((bs, FOX_HEADS, LANES - ts), F32)], axis=2)
        neg_f = _neg_cumsum_call(logf_all.reshape(bs * FOX_HEADS, past + LANES))
        b = _fox_sample_call(q, cache_fox_k[i].reshape(bs * past, FOX_WIDTH),
                             cache_fox_v[i].reshape(bs * past, FOX_WIDTH), k, v, neg_f, bs)
        hs = _merge_call(hs, a, b, m, gates, wpa, wpb, wpm, wo, tmix_s)
        hs = _mixer_call(hs, gf, mw1, mw3, mw2, tmix_s, tf, wr, gfin)
        outs["fk_s"].append(k.reshape(bs, ts, FOX_HEADS, FOX_HEAD_DIM))
        outs["fv_s"].append(v.reshape(bs, ts, FOX_HEADS, FOX_HEAD_DIM))
        outs["fl_s"].append(logf_t.transpose(0, 2, 1))
        outs["gv_s"].append(vn.reshape(bs, ts, GMLP_WIDTH))

    st = {k: jnp.stack(v) for k, v in outs.items()}
    return (hp.reshape(bp, sp, d), hs.reshape(bs, ts, d),
            st["fk_p"], st["fv_p"], st["fl_p"], st["mk_p"], st["mv_p"],
            st["fk_s"], st["fv_s"], st["fl_s"], st["gv_s"])
```

```python
import functools

import numpy as np
import jax
import jax.numpy as jnp
from jax import lax
from jax.experimental import pallas as pl
from jax.experimental.pallas import tpu as pltpu

F32 = jnp.float32
BF16 = jnp.bfloat16

EPS = 1e-6
GMLP_GROUPS = 8
GMLP_GROUP_DIM = 64
GMLP_WIDTH = GMLP_GROUPS * GMLP_GROUP_DIM
GMLP_BLOCK = 128
FOX_HEADS = 8
FOX_HEAD_DIM = 128
FOX_WIDTH = FOX_HEADS * FOX_HEAD_DIM
MEM_HEADS = 4
MEM_HEAD_DIM = 128
MEM_WIDTH = MEM_HEADS * MEM_HEAD_DIM
N_BRANCH = 3
TOP_K = 2
LANES = 128
BF16_ROWS = 16
LOG2E = 1.4426950408889634
VMEM_LIMIT = 56 * 1024 * 1024

_NT = (((1,), (1,)), ((), ()))


def _cparams(*sem):
    return pltpu.CompilerParams(dimension_semantics=sem, vmem_limit_bytes=VMEM_LIMIT)


def _rms(x, g):
    return x * lax.rsqrt(jnp.mean(x * x, axis=-1, keepdims=True) + EPS) * g


def _sigmoid(x):
    return 1.0 / (1.0 + jnp.exp(-x))


def _dot(a, b):
    return jnp.dot(a, b, preferred_element_type=F32)


def _dot_nt(a, b):
    return lax.dot_general(a, b, _NT, preferred_element_type=F32)


def _uv_kernel(h_ref, g_ref, w_ref, lng_ref, lnb_ref, ws_ref, bs_ref, a_ref, vn_ref):
    tm = h_ref.shape[0]
    xn = _rms(h_ref[...], g_ref[...]).astype(BF16)
    z = _dot(xn, w_ref[...])
    uv = 0.5 * z * (1.0 + lax.erf(z * np.float32(2.0 ** -0.5)))
    u = uv[:, :GMLP_WIDTH]
    v = uv[:, GMLP_WIDTH:]
    mu = jnp.mean(v, axis=-1, keepdims=True)
    vc = v - mu
    var = jnp.mean(vc * vc, axis=-1, keepdims=True)
    vn = vc * lax.rsqrt(var + EPS) * lng_ref[...] + lnb_ref[...]
    vn_ref[...] = vn
    lane = lax.broadcasted_iota(jnp.int32, (GMLP_BLOCK, LANES), 1)
    low = lane < GMLP_GROUP_DIM
    for r in range(tm // GMLP_BLOCK):
        rows = slice(r * GMLP_BLOCK, (r + 1) * GMLP_BLOCK)
        for p in range(GMLP_WIDTH // LANES):
            cols = slice(p * LANES, (p + 1) * LANES)
            vs = vn[rows, cols].astype(BF16)
            zero = jnp.zeros_like(vs)
            rhs = jnp.concatenate([jnp.where(low, vs, zero), jnp.where(low, zero, vs)], axis=0)
            mixed = _dot(ws_ref[p], rhs) + bs_ref[p]
            a_ref[rows, cols] = (u[rows, cols] * mixed).astype(a_ref.dtype)


def _uv_call(h, g, w_uv, ln_g, ln_b, ws_pairs, bs_pairs, tm):
    n, d = h.shape
    const = lambda *shape: pl.BlockSpec(shape, lambda t: (0,) * len(shape))
    return pl.pallas_call(
        _uv_kernel,
        grid=(n // tm,),
        in_specs=[
            pl.BlockSpec((tm, d), lambda t: (t, 0)),
            const(1, d),
            const(d, 2 * GMLP_WIDTH),
            const(1, GMLP_WIDTH),
            const(1, GMLP_WIDTH),
            const(*ws_pairs.shape),
            const(*bs_pairs.shape),
        ],
        out_specs=[
            pl.BlockSpec((tm, GMLP_WIDTH), lambda t: (t, 0)),
            pl.BlockSpec((tm, GMLP_WIDTH), lambda t: (t, 0)),
        ],
        out_shape=[
            jax.ShapeDtypeStruct((n, GMLP_WIDTH), BF16),
            jax.ShapeDtypeStruct((n, GMLP_WIDTH), F32),
        ],
        compiler_params=_cparams("parallel"),
        name="inproj_gmlp",
    )(h, g, w_uv, ln_g, ln_b, ws_pairs, bs_pairs)


def _qkv_kernel(h_ref, g_ref, w_ref, q_ref, k_ref, v_ref, *bf_refs, for_flash):
    xn = _rms(h_ref[...], g_ref[...]).astype(BF16)
    q = _dot(xn, w_ref[:, :FOX_WIDTH])
    k = _dot(xn, w_ref[:, FOX_WIDTH:2 * FOX_WIDTH])
    v = _dot(xn, w_ref[:, 2 * FOX_WIDTH:])
    k_ref[...] = k
    v_ref[...] = v
    if for_flash:
        kb_ref, vt_ref = bf_refs
        q_ref[...] = (q * np.float32(FOX_HEAD_DIM ** -0.5 * LOG2E)).T.astype(BF16)
        kb_ref[...] = k.astype(BF16)
        vt_ref[...] = v.T.astype(BF16)
    else:
        q_ref[...] = q.astype(BF16)


def _qkv_call(h, g, w_qkv, tm, for_flash):
    n, d = h.shape
    row = lambda width: pl.BlockSpec((tm, width), lambda t: (t, 0))
    col = pl.BlockSpec((FOX_WIDTH, tm), lambda t: (0, t))
    rows_f32 = jax.ShapeDtypeStruct((n, FOX_WIDTH), F32)
    rows_bf16 = jax.ShapeDtypeStruct((n, FOX_WIDTH), BF16)
    cols_bf16 = jax.ShapeDtypeStruct((FOX_WIDTH, n), BF16)
    if for_flash:
        out_specs = [col, row(FOX_WIDTH), row(FOX_WIDTH), row(FOX_WIDTH), col]
        out_shape = [cols_bf16, rows_f32, rows_f32, rows_bf16, cols_bf16]
    else:
        out_specs = [row(FOX_WIDTH)] * 3
        out_shape = [rows_bf16, rows_f32, rows_f32]
    return pl.pallas_call(
        functools.partial(_qkv_kernel, for_flash=for_flash),
        grid=(n // tm,),
        in_specs=[row(d), pl.BlockSpec((1, d), lambda t: (0, 0)),
                  pl.BlockSpec((d, 3 * FOX_WIDTH), lambda t: (0, 0))],
        out_specs=out_specs,
        out_shape=out_shape,
        compiler_params=_cparams("parallel"),
        name="inproj_qkv",
    )(h, g, w_qkv)


def _split3(x):
    hi = x.astype(BF16)
    r = x - hi.astype(F32)
    mid = r.astype(BF16)
    lo = (r - mid.astype(F32)).astype(BF16)
    return hi, mid, lo


def _fmg_kernel(h_ref, g_ref, wf_ref, bf_ref, wmq_ref, wg_ref, mk_ref, mv_ref, *rest,
                tiles_per_seq, bias_cols):
    if bias_cols:
        tri_ref, sel_ref, logf_ref, m_ref, gates_ref, kf_ref, carry_sc = rest
    else:
        logf_ref, m_ref, gates_ref = rest
    tm = h_ref.shape[0]
    xn = _rms(h_ref[...], g_ref[...]).astype(BF16)
    zf = _dot(xn, wf_ref[...]) + bf_ref[...]
    logf = jnp.minimum(zf, 0.0) - jnp.log1p(jnp.exp(-jnp.abs(zf)))
    logf_ref[...] = logf[:, :FOX_HEADS]
    if bias_cols:
        @pl.when(pl.program_id(0) % tiles_per_seq == 0)
        def _():
            carry_sc[...] = jnp.zeros_like(carry_sc)

        lane = lax.broadcasted_iota(jnp.int32, logf.shape, 1)
        x = jnp.where(lane < FOX_HEADS, logf, 0.0)
        tri = tri_ref[...]
        f = sum(_dot(tri, part) for part in _split3(x)) + carry_sc[...]
        carry_sc[...] = f[tm - 1:tm, :]
        kf = sum(_dot(part, sel_ref[c]) for c, part in enumerate(_split3(f * np.float32(-LOG2E))))
        kf_ref[...] = kf.astype(BF16)
    mq = _dot(xn, wmq_ref[...]).astype(BF16)
    for hh in range(MEM_HEADS):
        cols = slice(hh * MEM_HEAD_DIM, (hh + 1) * MEM_HEAD_DIM)
        s = _dot_nt(mq[:, cols], mk_ref[:, cols].astype(BF16)) * np.float32(MEM_HEAD_DIM ** -0.5)
        p = jnp.exp(s - jnp.max(s, axis=-1, keepdims=True))
        p = (p / jnp.sum(p, axis=-1, keepdims=True)).astype(BF16)
        m_ref[:, cols] = _dot(p, mv_ref[:, cols].astype(BF16)).astype(m_ref.dtype)
    gates_ref[...] = _sigmoid(_dot(xn, wg_ref[...])).astype(gates_ref.dtype)


def _fmg_call(h, g, w_f, b_f, w_mq, w_g, mk, mv, batch, tm, gates_dtype, bias_cols):
    n, d = h.shape
    seq = n // batch
    tpb = seq // tm
    n_mem = mk.shape[0] // batch
    const = lambda *shape: pl.BlockSpec(shape, lambda t: (0,) * len(shape))
    row = lambda width: pl.BlockSpec((tm, width), lambda t: (t, 0))
    args = [h, g, w_f, b_f, w_mq, w_g, mk, mv]
    in_specs = [
        row(d), const(1, d), const(d, LANES), const(1, LANES), const(d, MEM_WIDTH),
        const(d, N_BRANCH * d),
        pl.BlockSpec((n_mem, MEM_WIDTH), lambda t: (t // tpb, 0)),
        pl.BlockSpec((n_mem, MEM_WIDTH), lambda t: (t // tpb, 0)),
    ]
    out_specs = [row(FOX_HEADS), row(MEM_WIDTH), row(N_BRANCH * d)]
    out_shape = [
        jax.ShapeDtypeStruct((n, FOX_HEADS), F32),
        jax.ShapeDtypeStruct((n, MEM_WIDTH), BF16),
        jax.ShapeDtypeStruct((n, N_BRANCH * d), gates_dtype),
    ]
    scratch = []
    if bias_cols:
        tri = jnp.tril(jnp.ones((tm, tm), BF16))
        head = jnp.arange(FOX_HEADS)
        sel = jnp.zeros((3, LANES, FOX_WIDTH), BF16)
        for c in range(3):
            sel = sel.at[c, head, head * FOX_HEAD_DIM + c].set(1)
        args += [tri, sel]
        in_specs += [const(tm, tm), const(3, LANES, FOX_WIDTH)]
        out_specs.append(row(FOX_WIDTH))
        out_shape.append(jax.ShapeDtypeStruct((n, FOX_WIDTH), BF16))
        scratch.append(pltpu.VMEM((1, LANES), F32))
    return pl.pallas_call(
        functools.partial(_fmg_kernel, tiles_per_seq=tpb, bias_cols=bias_cols),
        grid=(n // tm,),
        in_specs=in_specs,
        out_specs=out_specs,
        out_shape=out_shape,
        scratch_shapes=scratch,
        compiler_params=_cparams("arbitrary" if bias_cols else "parallel"),
        name="inproj_logf_mem_gates",
    )(*args)


def _memkv_kernel(x_ref, g_ref, w_ref, mk_ref, mv_ref):
    xn = _rms(x_ref[...], g_ref[...]).astype(BF16)
    mk_ref[...] = _dot(xn, w_ref[:, :MEM_WIDTH])
    mv_ref[...] = _dot(xn, w_ref[:, MEM_WIDTH:])


def _memkv_call(mem, g, w_kv, tm):
    n, d = mem.shape
    return pl.pallas_call(
        _memkv_kernel,
        grid=(n // tm,),
        in_specs=[pl.BlockSpec((tm, d), lambda t: (t, 0)),
                  pl.BlockSpec((1, d), lambda t: (0, 0)),
                  pl.BlockSpec((d, 2 * MEM_WIDTH), lambda t: (0, 0))],
        out_specs=[pl.BlockSpec((tm, MEM_WIDTH), lambda t: (t, 0))] * 2,
        out_shape=[jax.ShapeDtypeStruct((n, MEM_WIDTH), F32)] * 2,
        compiler_params=_cparams("parallel"),
        name="memory_kv",
    )(mem, g, w_kv)


def _neg_cumsum_kernel(x_ref, o_ref):
    x = x_ref[...]
    width = x.shape[-1]
    lane = lax.broadcasted_iota(jnp.int32, x.shape, 1)
    d = 1
    while d < width:
        x = x + jnp.where(lane >= d, pltpu.roll(x, d, 1), 0.0)
        d *= 2
    o_ref[...] = -x


def _neg_cumsum_call(x):
    return pl.pallas_call(
        _neg_cumsum_kernel,
        out_shape=jax.ShapeDtypeStruct(x.shape, F32),
        compiler_params=pltpu.CompilerParams(vmem_limit_bytes=VMEM_LIMIT),
        name="logf_cumsum",
    )(x)


def _fox_kernel(qi_ref, kj_ref, flag_ref, qt_ref, ones_ref, k_ref, kf_ref, vt_ref, o_ref,
                m_sc, acc_sc, *, tile, group):
    step = pl.program_id(2)
    flags = flag_ref[step]

    @pl.when(flags & 1 != 0)
    def _():
        m_sc[...] = jnp.full_like(m_sc, -jnp.inf)
        acc_sc[...] = jnp.zeros_like(acc_sc)

    def update(diagonal):
        ka = jnp.concatenate([k_ref[...], kf_ref[...]], axis=1)
        va = jnp.concatenate([vt_ref[...], ones_ref[:BF16_ROWS]], axis=0)
        width = group if diagonal else tile
        for gi in range(tile // width):
            cols = slice(gi * width, (gi + 1) * width)
            rows = (gi + 1) * width
            qa = jnp.concatenate([qt_ref[:, cols], ones_ref[:, :width]], axis=0)
            s = _dot(ka[:rows], qa)
            if diagonal:
                key = lax.broadcasted_iota(jnp.int32, s.shape, 0)
                qry = lax.broadcasted_iota(jnp.int32, s.shape, 1) + gi * width
                s = jnp.where(key <= qry, s, -jnp.inf)
            m_prev = m_sc[:, cols]
            m_new = jnp.maximum(m_prev, jnp.max(s, axis=0, keepdims=True))
            p = jnp.exp2(s - m_new).astype(BF16)
            acc_sc[:, cols] = jnp.exp2(m_prev - m_new) * acc_sc[:, cols] + _dot(va[:, :rows], p)
            m_sc[:, cols] = m_new

    @pl.when(flags & 2 == 0)
    def _():
        update(False)

    @pl.when(flags & 2 != 0)
    def _():
        update(True)
        acc = acc_sc[...]
        out = acc[:FOX_HEAD_DIM] * (1.0 / acc[FOX_HEAD_DIM:FOX_HEAD_DIM + 1])
        o_ref[...] = out.T.astype(o_ref.dtype)


def _fox_tables(n_tiles):
    qi, kj, flags = [], [], []
    for i in range(n_tiles):
        for j in range(i + 1):
            qi.append(i)
            kj.append(j)
            flags.append((1 if j == 0 else 0) | (2 if j == i else 0))
    return (np.asarray(qi, np.int32), np.asarray(kj, np.int32), np.asarray(flags, np.int32))


def _fox_call(q_t, k, kf, v_t, batch, tile, group):
    width, n = q_t.shape
    seq = n // batch
    heads = width // FOX_HEAD_DIM
    nt = seq // tile
    qi, kj, flags = _fox_tables(nt)
    ones = jnp.zeros((FOX_HEAD_DIM, tile), BF16).at[:3].set(1)
    q_map = lambda b, h, s, qi, kj, fl: (h, b * nt + qi[s])
    kt_map = lambda b, h, s, qi, kj, fl: (h, b * nt + kj[s])
    k_map = lambda b, h, s, qi, kj, fl: (b * nt + kj[s], h)
    grid_spec = pltpu.PrefetchScalarGridSpec(
        num_scalar_prefetch=3,
        grid=(batch, heads, len(qi)),
        in_specs=[
            pl.BlockSpec((FOX_HEAD_DIM, tile), q_map),
            pl.BlockSpec((FOX_HEAD_DIM, tile), lambda b, h, s, qi, kj, fl: (0, 0)),
            pl.BlockSpec((tile, FOX_HEAD_DIM), k_map),
            pl.BlockSpec((tile, FOX_HEAD_DIM), k_map),
            pl.BlockSpec((FOX_HEAD_DIM, tile), kt_map),
        ],
        out_specs=pl.BlockSpec((tile, FOX_HEAD_DIM), lambda b, h, s, qi, kj, fl: (b * nt + qi[s], h)),
        scratch_shapes=[
            pltpu.VMEM((1, tile), F32),
            pltpu.VMEM((FOX_HEAD_DIM + BF16_ROWS, tile), F32),
        ],
    )
    return pl.pallas_call(
        functools.partial(_fox_kernel, tile=tile, group=group),
        grid_spec=grid_spec,
        out_shape=jax.ShapeDtypeStruct((n, width), BF16),
        compiler_params=_cparams("parallel", "parallel", "arbitrary"),
        name="fox_attention",
    )(jnp.asarray(qi), jnp.asarray(kj), jnp.asarray(flags), q_t, ones, k, kf, v_t)


def _fox_sample_kernel(q_ref, kc_ref, vc_ref, kn_ref, vn_ref, nf_ref, o_ref):
    t = q_ref.shape[0]
    past = kc_ref.shape[0]
    q = q_ref[...]
    nf = nf_ref[0]
    pad = jnp.zeros((LANES - t, FOX_HEAD_DIM), BF16)
    kn = jnp.concatenate([kn_ref[...].astype(BF16), pad], axis=0)
    vn = jnp.concatenate([vn_ref[...].astype(BF16), pad], axis=0)
    scale = np.float32(FOX_HEAD_DIM ** -0.5)
    s_c = _dot_nt(q, kc_ref[...].astype(BF16)) * scale + nf[:, :past]
    s_n = _dot_nt(q, kn) * scale + nf[:, past:]
    row = lax.broadcasted_iota(jnp.int32, (t, LANES), 0)
    col = lax.broadcasted_iota(jnp.int32, (t, LANES), 1)
    s_n = jnp.where(col <= row, s_n, -jnp.inf)
    m = jnp.maximum(jnp.max(s_c, axis=-1, keepdims=True), jnp.max(s_n, axis=-1, keepdims=True))
    p_c = jnp.exp(s_c - m)
    p_n = jnp.exp(s_n - m)
    l = jnp.sum(p_c, axis=-1, keepdims=True) + jnp.sum(p_n, axis=-1, keepdims=True)
    p_c = (p_c / l).astype(BF16)
    p_n = (p_n / l).astype(BF16)
    o_ref[...] = (_dot(p_c, vc_ref[...].astype(BF16)) + _dot(p_n, vn)).astype(o_ref.dtype)


def _fox_sample_call(q, k_cache, v_cache, k_new, v_new, neg_f, batch):
    n, width = q.shape
    t = n // batch
    heads = width // FOX_HEAD_DIM
    past = k_cache.shape[0] // batch
    nf = neg_f.reshape(batch * heads, 1, past + LANES)
    new = pl.BlockSpec((t, FOX_HEAD_DIM), lambda b, h: (b, h))
    old = pl.BlockSpec((past, FOX_HEAD_DIM), lambda b, h: (b, h))
    return pl.pallas_call(
        _fox_sample_kernel,
        grid=(batch, heads),
        in_specs=[new, old, old, new, new,
                  pl.BlockSpec((1, 1, past + LANES), lambda b, h: (b * heads + h, 0, 0))],
        out_specs=new,
        out_shape=jax.ShapeDtypeStruct((n, width), BF16),
        compiler_params=_cparams("parallel", "parallel"),
        name="fox_attention_sample",
    )(q, k_cache, v_cache, k_new, v_new, nf)


def _merge_kernel(h_ref, a_ref, b_ref, m_ref, gates_ref, wpa_ref, wpb_ref, wpm_ref, wo_ref, o_ref):
    d = h_ref.shape[1]
    y = gates_ref[:, :d].astype(F32) * _dot(a_ref[...], wpa_ref[...])
    y = y + gates_ref[:, d:2 * d].astype(F32) * _dot(b_ref[...], wpb_ref[...])
    y = y + gates_ref[:, 2 * d:].astype(F32) * _dot(m_ref[...], wpm_ref[...])
    o_ref[...] = h_ref[...] + _dot(y.astype(BF16), wo_ref[...])


def _merge_call(h, a, b, m, gates, w_pa, w_pb, w_pm, w_o, tm):
    n, d = h.shape
    row = lambda width: pl.BlockSpec((tm, width), lambda t: (t, 0))
    const = lambda w: pl.BlockSpec(w.shape, lambda t: (0, 0))
    return pl.pallas_call(
        _merge_kernel,
        grid=(n // tm,),
        in_specs=[row(d), row(GMLP_WIDTH), row(FOX_WIDTH), row(MEM_WIDTH), row(N_BRANCH * d),
                  const(w_pa), const(w_pb), const(w_pm), const(w_o)],
        out_specs=row(d),
        out_shape=jax.ShapeDtypeStruct((n, d), F32),
        compiler_params=_cparams("parallel"),
        name="merge_branches",
    )(h, a, b, m, gates, w_pa, w_pb, w_pm, w_o)


def _top2(logits, n_exp):
    lane = lax.broadcasted_iota(jnp.int32, logits.shape, 1)
    neg = jnp.float32(-jnp.inf)
    logits = jnp.where(lane < n_exp, logits, neg)
    v1 = jnp.max(logits, axis=-1, keepdims=True)
    i1 = jnp.min(jnp.where(logits == v1, lane, LANES), axis=-1, keepdims=True)
    rest = jnp.where(lane == i1, neg, logits)
    v2 = jnp.max(rest, axis=-1, keepdims=True)
    i2 = jnp.min(jnp.where(rest == v2, lane, LANES), axis=-1, keepdims=True)
    e2 = jnp.exp(v2 - v1)
    return lane, i1, i2, 1.0 / (1.0 + e2), e2 / (1.0 + e2)


def _top2_combine(logits, n_exp):
    lane, i1, i2, p1, p2 = _top2(logits, n_exp)
    return jnp.where(lane == i1, p1, 0.0) + jnp.where(lane == i2, p2, 0.0)


def _mixer_kernel(h_ref, g_ref, *refs, routed, final_norm):
    refs = list(refs)
    wr_ref = refs.pop(0) if routed else None
    gfin_ref = refs.pop(0) if final_norm else None
    w1_ref, w3_ref, w2_ref, o_ref, xn_sc, acc_sc = refs[:6]
    comb_sc = refs[6] if routed else None
    e = pl.program_id(1)
    c = pl.program_id(2)
    first = jnp.logical_and(e == 0, c == 0)
    last = jnp.logical_and(e == pl.num_programs(1) - 1, c == pl.num_programs(2) - 1)

    @pl.when(first)
    def _():
        xn = _rms(h_ref[...], g_ref[...])
        xn_sc[...] = xn.astype(BF16)
        acc_sc[...] = jnp.zeros_like(acc_sc)
        if routed:
            logits = _dot(xn.astype(BF16), wr_ref[...])
            comb_sc[...] = _top2_combine(logits, pl.num_programs(1))

    xb = xn_sc[...]
    g1 = _dot(xb, w1_ref[0])
    g3 = _dot(xb, w3_ref[0])
    mid = (g1 * _sigmoid(g1) * g3).astype(BF16)
    y = _dot(mid, w2_ref[0])
    if routed:
        lane = lax.broadcasted_iota(jnp.int32, comb_sc.shape, 1)
        y = y * jnp.sum(jnp.where(lane == e, comb_sc[...], 0.0), axis=-1, keepdims=True)
    acc_sc[...] += y

    @pl.when(last)
    def _():
        out = h_ref[...] + acc_sc[...]
        if final_norm:
            out = _rms(out, gfin_ref[...])
        o_ref[...] = out


def _mixer_call(h, g, w1, w3, w2, tm, tf, w_router=None, g_final=None):
    n, d = h.shape
    n_exp, _, ff = w1.shape
    routed = w_router is not None
    final_norm = g_final is not None
    row = pl.BlockSpec((tm, d), lambda t, e, c: (t, 0))
    vec = pl.BlockSpec((1, d), lambda t, e, c: (0, 0))
    args, in_specs = [h, g], [row, vec]
    if routed:
        args.append(w_router)
        in_specs.append(pl.BlockSpec(w_router.shape, lambda t, e, c: (0, 0)))
    if final_norm:
        args.append(g_final)
        in_specs.append(vec)
    args += [w1, w3, w2]
    in_specs += [
        pl.BlockSpec((1, d, tf), lambda t, e, c: (e, 0, c)),
        pl.BlockSpec((1, d, tf), lambda t, e, c: (e, 0, c)),
        pl.BlockSpec((1, tf, d), lambda t, e, c: (e, c, 0)),
    ]
    scratch = [pltpu.VMEM((tm, d), BF16), pltpu.VMEM((tm, d), F32)]
    if routed:
        scratch.append(pltpu.VMEM((tm, LANES), F32))
    return pl.pallas_call(
        functools.partial(_mixer_kernel, routed=routed, final_norm=final_norm),
        grid=(n // tm, n_exp, ff // tf),
        in_specs=in_specs,
        out_specs=row,
        out_shape=jax.ShapeDtypeStruct((n, d), F32),
        scratch_shapes=scratch,
        compiler_params=_cparams("parallel", "arbitrary", "arbitrary"),
        name="channel_mixer",
    )(*args)


def _router_kernel(h_ref, g_ref, wr_ref, idx_ref, p1_ref, p2_ref, *, n_exp):
    xn = _rms(h_ref[...], g_ref[...]).astype(BF16)
    lane, i1, i2, p1, p2 = _top2(_dot(xn, wr_ref[...]), n_exp)
    idx_ref[...] = jnp.where(lane == 0, i1, jnp.where(lane == 1, i2, 0))
    p1_ref[...] = jnp.broadcast_to(p1, p1_ref.shape)
    p2_ref[...] = jnp.broadcast_to(p2, p2_ref.shape)


def _router_call(h, g, w_router, n_exp, tm):
    n, d = h.shape
    wide = pl.BlockSpec((tm, LANES), lambda t: (t, 0))
    return pl.pallas_call(
        functools.partial(_router_kernel, n_exp=n_exp),
        grid=(n // tm,),
        in_specs=[pl.BlockSpec((tm, d), lambda t: (t, 0)), pl.BlockSpec((1, d), lambda t: (0, 0)),
                  pl.BlockSpec((d, LANES), lambda t: (0, 0))],
        out_specs=[wide, wide, wide],
        out_shape=[jax.ShapeDtypeStruct((n, LANES), jnp.int32),
                   jax.ShapeDtypeStruct((n, LANES), F32),
                   jax.ShapeDtypeStruct((n, LANES), F32)],
        compiler_params=_cparams("parallel"),
        name="moe_router",
    )(h, g, w_router)


def _row_copy(src_ref, src_row, dst_ref, dst_row, sem):
    return pltpu.make_async_copy(src_ref.at[pl.ds(src_row, 1)], dst_ref.at[pl.ds(dst_row, 1)], sem)


def _dispatch_kernel(pos1_ref, pos2_ref, h_ref, g_ref, zero_ref, xs_ref, x_sc, sem):
    del zero_ref
    tm = h_ref.shape[0]
    base = pl.program_id(0) * tm
    x_sc[...] = _rms(h_ref[...], g_ref[...])

    def start(i, carry):
        _row_copy(x_sc, i, xs_ref, pos1_ref[base + i], sem).start()
        _row_copy(x_sc, i, xs_ref, pos2_ref[base + i], sem).start()
        return carry

    def wait(i, carry):
        _row_copy(x_sc, i, xs_ref, pos1_ref[base + i], sem).wait()
        _row_copy(x_sc, i, xs_ref, pos2_ref[base + i], sem).wait()
        return carry

    lax.fori_loop(0, tm, start, 0)
    lax.fori_loop(0, tm, wait, 0)


def _dispatch_call(h, g, pos1, pos2, n_rows, tm):
    n, d = h.shape
    grid_spec = pltpu.PrefetchScalarGridSpec(
        num_scalar_prefetch=2,
        grid=(n // tm,),
        in_specs=[pl.BlockSpec((tm, d), lambda t, p1, p2: (t, 0)),
                  pl.BlockSpec((1, d), lambda t, p1, p2: (0, 0)),
                  pl.BlockSpec(memory_space=pl.ANY)],
        out_specs=pl.BlockSpec(memory_space=pl.ANY),
        scratch_shapes=[pltpu.VMEM((tm, d), F32), pltpu.SemaphoreType.DMA(())],
    )
    return pl.pallas_call(
        _dispatch_kernel,
        grid_spec=grid_spec,
        out_shape=jax.ShapeDtypeStruct((n_rows, d), F32),
        input_output_aliases={4: 0},
        compiler_params=_cparams("arbitrary"),
        name="moe_dispatch",
    )(pos1, pos2, h, g, jnp.zeros((n_rows, d), F32))


def _experts_kernel(te_ref, nact_ref, x_ref, w1_ref, w3_ref, w2_ref, y_ref, xb_sc, acc_sc):
    t = pl.program_id(0)
    c = pl.program_id(1)

    @pl.when(c == 0)
    def _():
        xb_sc[...] = x_ref[...].astype(BF16)
        acc_sc[...] = jnp.zeros_like(acc_sc)

    @pl.when(t < nact_ref[0])
    def _():
        xb = xb_sc[...]
        g1 = _dot(xb, w1_ref[0])
        g3 = _dot(xb, w3_ref[0])
        mid = (g1 * _sigmoid(g1) * g3).astype(BF16)
        acc_sc[...] += _dot(mid, w2_ref[0])

    @pl.when(c == pl.num_programs(1) - 1)
    def _():
        y_ref[...] = acc_sc[...]


def _experts_call(xs, tile_expert, n_active, w1, w3, w2, tm, tf):
    n_rows, d = xs.shape
    ff = w1.shape[2]
    nff = ff // tf

    def chunk(t, c, te, nact):
        return jnp.where(t < nact[0], c, nff - 1)

    grid_spec = pltpu.PrefetchScalarGridSpec(
        num_scalar_prefetch=2,
        grid=(n_rows // tm, nff),
        in_specs=[
            pl.BlockSpec((tm, d), lambda t, c, te, nact: (t, 0)),
            pl.BlockSpec((1, d, tf), lambda t, c, te, nact: (te[t], 0, chunk(t, c, te, nact))),
            pl.BlockSpec((1, d, tf), lambda t, c, te, nact: (te[t], 0, chunk(t, c, te, nact))),
            pl.BlockSpec((1, tf, d), lambda t, c, te, nact: (te[t], chunk(t, c, te, nact), 0)),
        ],
        out_specs=pl.BlockSpec((tm, d), lambda t, c, te, nact: (t, 0)),
        scratch_shapes=[pltpu.VMEM((tm, d), BF16), pltpu.VMEM((tm, d), F32)],
    )
    return pl.pallas_call(
        _experts_kernel,
        grid_spec=grid_spec,
        out_shape=jax.ShapeDtypeStruct((n_rows, d), F32),
        compiler_params=_cparams("parallel", "arbitrary"),
        name="moe_experts",
    )(tile_expert, n_active, xs, w1, w3, w2)


def _combine_kernel(pos1_ref, pos2_ref, h_ref, p1_ref, p2_ref, *refs, final_norm):
    refs = list(refs)
    gfin_ref = refs.pop(0) if final_norm else None
    ys_ref, o_ref, y1_sc, y2_sc, sem = refs
    tm, d = h_ref.shape
    base = pl.program_id(0) * tm

    def start(i, carry):
        _row_copy(ys_ref, pos1_ref[base + i], y1_sc, i, sem).start()
        _row_copy(ys_ref, pos2_ref[base + i], y2_sc, i, sem).start()
        return carry

    def wait(i, carry):
        _row_copy(ys_ref, pos1_ref[base + i], y1_sc, i, sem).wait()
        _row_copy(ys_ref, pos2_ref[base + i], y2_sc, i, sem).wait()
        return carry

    lax.fori_loop(0, tm, start, 0)
    lax.fori_loop(0, tm, wait, 0)
    w1 = jnp.concatenate([p1_ref[...]] * (d // LANES), axis=1)
    w2 = jnp.concatenate([p2_ref[...]] * (d // LANES), axis=1)
    out = h_ref[...] + (w1 * y1_sc[...] + w2 * y2_sc[...])
    if final_norm:
        out = _rms(out, gfin_ref[...])
    o_ref[...] = out


def _combine_call(h, p1, p2, ys, pos1, pos2, tm, g_final=None):
    n, d = h.shape
    final_norm = g_final is not None
    row = pl.BlockSpec((tm, d), lambda t, a, b: (t, 0))
    wide = pl.BlockSpec((tm, LANES), lambda t, a, b: (t, 0))
    args, in_specs = [h, p1, p2], [row, wide, wide]
    if final_norm:
        args.append(g_final)
        in_specs.append(pl.BlockSpec((1, d), lambda t, a, b: (0, 0)))
    args.append(ys)
    in_specs.append(pl.BlockSpec(memory_space=pl.ANY))
    grid_spec = pltpu.PrefetchScalarGridSpec(
        num_scalar_prefetch=2,
        grid=(n // tm,),
        in_specs=in_specs,
        out_specs=row,
        scratch_shapes=[pltpu.VMEM((tm, d), F32), pltpu.VMEM((tm, d), F32),
                        pltpu.SemaphoreType.DMA(())],
    )
    return pl.pallas_call(
        functools.partial(_combine_kernel, final_norm=final_norm),
        grid_spec=grid_spec,
        out_shape=jax.ShapeDtypeStruct((n, d), F32),
        compiler_params=_cparams("arbitrary"),
        name="moe_combine",
    )(pos1, pos2, *args)


def _routing_tables(idx, n_exp, tm):
    n = idx.shape[0]
    e = idx[:, :TOP_K].reshape(-1)
    onehot = (e[:, None] == jnp.arange(n_exp, dtype=jnp.int32)[None, :]).astype(jnp.int32)
    csum = jnp.cumsum(onehot, axis=0)
    rank = jnp.take_along_axis(csum, e[:, None], axis=1)[:, 0] - 1
    counts = csum[-1]
    padded = (counts + tm - 1) // tm * tm
    seg_end = jnp.cumsum(padded)
    pos = ((seg_end - padded)[e] + rank).reshape(n, TOP_K)
    n_rows = TOP_K * n + n_exp * tm
    tile_first = jnp.arange(n_rows // tm, dtype=jnp.int32) * tm
    tile_expert = jnp.minimum(jnp.sum(tile_first[:, None] >= seg_end[None, :], axis=1), n_exp - 1)
    n_active = (seg_end[-1] // tm).reshape(1)
    return pos[:, 0], pos[:, 1], tile_expert.astype(jnp.int32), n_active.astype(jnp.int32), n_rows


def _routed_moe(h, g, w_router, w1, w3, w2, tm_tok, tm_rows, tf, g_final):
    n_exp = w1.shape[0]
    idx, p1, p2 = _router_call(h, g, w_router, n_exp, tm_tok)
    pos1, pos2, tile_expert, n_active, n_rows = _routing_tables(idx, n_exp, tm_rows)
    xs = _dispatch_call(h, g, pos1, pos2, n_rows, tm_tok)
    ys = _experts_call(xs, tile_expert, n_active, w1, w3, w2, tm_rows, tf)
    return _combine_call(h, p1, p2, ys, pos1, pos2, tm_tok, g_final)


def _largest_tile(n, cap, quantum):
    best = quantum
    t = quantum
    while t <= min(n, cap):
        if n % t == 0:
            best = t
        t += quantum
    return best


def _spatial_operands(w_s, b_s, blk):
    reps = GMLP_BLOCK // blk
    w = w_s[:, :blk, :blk] * jnp.tril(jnp.ones((blk, blk), w_s.dtype))
    eye = jnp.eye(reps, dtype=w_s.dtype)
    w = jnp.einsum("ab,gts->gatbs", eye, w).reshape(GMLP_GROUPS, GMLP_BLOCK, GMLP_BLOCK)
    b = jnp.tile(b_s[:, :blk], (1, reps))
    ws_pairs = jnp.concatenate([w[0::2], w[1::2]], axis=2).astype(BF16)
    bs_pairs = jnp.concatenate(
        [jnp.broadcast_to(b[0::2, :, None], (GMLP_GROUPS // 2, GMLP_BLOCK, GMLP_GROUP_DIM)),
         jnp.broadcast_to(b[1::2, :, None], (GMLP_GROUPS // 2, GMLP_BLOCK, GMLP_GROUP_DIM))],
        axis=2).astype(F32)
    return ws_pairs, bs_pairs


def kernel(x_prompt, x_sample, cache_fox_k, cache_fox_v, cache_fox_logf, cache_mem_k, cache_mem_v, mem_prompt, g_mix, w_in, b_f, ln_v_g, ln_v_b, w_s, b_s, g_mem, w_mem_kv, w_pa, w_pb, w_pm, w_o, g_ffn, w1, w3, w2, w_router, e_w1, e_w3, e_w2, g_final):
    depth = w_in.shape[0]
    bp, sp, d = x_prompt.shape
    bs, ts, _ = x_sample.shape
    past = cache_fox_k.shape[2]
    n_mem = mem_prompt.shape[1]
    n_exp = e_w1.shape[1]
    assert sp % GMLP_BLOCK == 0 and GMLP_BLOCK % ts == 0 and (bs * ts) % GMLP_BLOCK == 0

    off_q = 2 * GMLP_WIDTH
    off_f = off_q + 3 * FOX_WIDTH
    off_mq = off_f + FOX_HEADS
    off_g = off_mq + MEM_WIDTH

    hp = x_prompt.reshape(bp * sp, d)
    hs = x_sample.reshape(bs * ts, d)
    mem = mem_prompt.reshape(bp * n_mem, d)

    tm_p = _largest_tile(sp, 512, GMLP_BLOCK)
    tm_s = GMLP_BLOCK
    fox_group = 4 * LANES
    fox_tile = _largest_tile(sp, 1024, fox_group)
    tmix_p = _largest_tile(bp * sp, 1024, LANES)
    tmix_s = bs * ts

    outs = {k: [] for k in ("fk_p", "fv_p", "fl_p", "mk_p", "mv_p", "fk_s", "fv_s", "fl_s", "gv_s")}
    for i in range(depth):
        g = g_mix[i].reshape(1, d)
        wi = w_in[i]
        w_uv = wi[:, :off_q].astype(BF16)
        w_qkv = wi[:, off_q:off_f].astype(BF16)
        w_f = jnp.zeros((d, LANES), BF16).at[:, :FOX_HEADS].set(wi[:, off_f:off_mq].astype(BF16))
        w_mq = wi[:, off_mq:off_g].astype(BF16)
        w_g = wi[:, off_g:].astype(BF16)
        bfi = jnp.zeros((1, LANES), F32).at[0, :FOX_HEADS].set(b_f[i])
        lng = ln_v_g[i].reshape(1, GMLP_WIDTH)
        lnb = ln_v_b[i].reshape(1, GMLP_WIDTH)
        wpa, wpb, wpm, wo = (w[i].astype(BF16) for w in (w_pa, w_pb, w_pm, w_o))
        gf = g_ffn[i].reshape(1, d)
        j = i // 2
        if i % 2 == 0:
            mw1, mw3, mw2 = (w[j][None].astype(BF16) for w in (w1, w3, w2))
            wr = None
        else:
            mw1, mw3, mw2 = (w[j].astype(BF16) for w in (e_w1, e_w3, e_w2))
            wr = jnp.zeros((d, LANES), BF16).at[:, :n_exp].set(w_router[j].astype(BF16))
        ff = mw1.shape[2]
        tf = _largest_tile(ff, 1408, LANES)
        gfin = g_final.reshape(1, d) if i == depth - 1 else None

        ws_pairs, bs_pairs = _spatial_operands(w_s[i], b_s[i], GMLP_BLOCK)
        mk, mv = _memkv_call(mem, g_mem[i].reshape(1, d), w_mem_kv[i].astype(BF16), n_mem)
        a, _ = _uv_call(hp, g, w_uv, lng, lnb, ws_pairs, bs_pairs, tm_p)
        q_t, k, v, kb, v_t = _qkv_call(hp, g, w_qkv, tm_p, True)
        logf, m, gates, kf = _fmg_call(hp, g, w_f, bfi, w_mq, w_g, mk, mv, bp, tm_p, BF16, True)
        b = _fox_call(q_t, kb, kf, v_t, bp, fox_tile, fox_group)
        hp = _merge_call(hp, a, b, m, gates, wpa, wpb, wpm, wo, tm_p)
        if wr is None:
            hp = _mixer_call(hp, gf, mw1, mw3, mw2, tmix_p, tf, None, gfin)
        else:
            hp = _routed_moe(hp, gf, wr, mw1, mw3, mw2, tm_p, tmix_p, tf, gfin)
        outs["fk_p"].append(k.reshape(bp, sp, FOX_HEADS, FOX_HEAD_DIM))
        outs["fv_p"].append(v.reshape(bp, sp, FOX_HEADS, FOX_HEAD_DIM))
        outs["fl_p"].append(logf.reshape(bp, sp, FOX_HEADS))
        outs["mk_p"].append(mk.reshape(bp, n_mem, MEM_HEADS, MEM_HEAD_DIM))
        outs["mv_p"].append(mv.reshape(bp, n_mem, MEM_HEADS, MEM_HEAD_DIM))

        ws_pairs, bs_pairs = _spatial_operands(w_s[i], b_s[i], ts)
        a, vn = _uv_call(hs, g, w_uv, lng, lnb, ws_pairs, bs_pairs, tm_s)
        q, k, v = _qkv_call(hs, g, w_qkv, bs * ts, False)
        logf, m, gates = _fmg_call(hs, g, w_f, bfi, w_mq, w_g,
                                   cache_mem_k[i].reshape(bs * n_mem, MEM_WIDTH),
                                   cache_mem_v[i].reshape(bs * n_mem, MEM_WIDTH), bs, ts, F32, False)
        logf = logf.reshape(bs, ts, FOX_HEADS)
        logf_all = jnp.concatenate(
            [cache_fox_logf[i].transpose(0, 2, 1), logf.transpose(0, 2, 1),
             jnp.zeros((bs, FOX_HEADS, LANES - ts), F32)], axis=2)
        neg_f = _neg_cumsum_call(logf_all.reshape(bs * FOX_HEADS, past + LANES))
        b = _fox_sample_call(q, cache_fox_k[i].reshape(bs * past, FOX_WIDTH),
                             cache_fox_v[i].reshape(bs * past, FOX_WIDTH), k, v, neg_f, bs)
        hs = _merge_call(hs, a, b, m, gates, wpa, wpb, wpm, wo, tmix_s)
        hs = _mixer_call(hs, gf, mw1, mw3, mw2, tmix_s, tf, wr, gfin)
        outs["fk_s"].append(k.reshape(bs, ts, FOX_HEADS, FOX_HEAD_DIM))
        outs["fv_s"].append(v.reshape(bs, ts, FOX_HEADS, FOX_HEAD_DIM))
        outs["fl_s"].append(logf)
        outs["gv_s"].append(vn.reshape(bs, ts, GMLP_WIDTH))

    st = {k: jnp.stack(v) for k, v in outs.items()}
    return (hp.reshape(bp, sp, d), hs.reshape(bs, ts, d),
            st["fk_p"], st["fv_p"], st["fl_p"], st["mk_p"], st["mv_p"],
            st["fk_s"], st["fv_s"], st["fl_s"], st["gv_s"])
```

```python
import functools

import numpy as np
import jax
import jax.numpy as jnp
from jax import lax
from jax.experimental import pallas as pl
from jax.experimental.pallas import tpu as pltpu

F32 = jnp.float32
BF16 = jnp.bfloat16

EPS = 1e-6
GMLP_GROUPS = 8
GMLP_GROUP_DIM = 64
GMLP_WIDTH = GMLP_GROUPS * GMLP_GROUP_DIM
GMLP_BLOCK = 128
FOX_HEADS = 8
FOX_HEAD_DIM = 128
FOX_WIDTH = FOX_HEADS * FOX_HEAD_DIM
MEM_HEADS = 4
MEM_HEAD_DIM = 128
MEM_WIDTH = MEM_HEADS * MEM_HEAD_DIM
N_BRANCH = 3
TOP_K = 2
LANES = 128
BF16_ROWS = 16
LOG2E = 1.4426950408889634
VMEM_LIMIT = 56 * 1024 * 1024

_NT = (((1,), (1,)), ((), ()))


def _cparams(*sem):
    return pltpu.CompilerParams(dimension_semantics=sem, vmem_limit_bytes=VMEM_LIMIT)


def _rms(x, g):
    return x * lax.rsqrt(jnp.mean(x * x, axis=-1, keepdims=True) + EPS) * g


def _sigmoid(x):
    return 1.0 / (1.0 + jnp.exp(-x))


def _dot(a, b):
    return jnp.dot(a, b, preferred_element_type=F32)


def _dot_nt(a, b):
    return lax.dot_general(a, b, _NT, preferred_element_type=F32)


def _uv_kernel(h_ref, g_ref, w_ref, lng_ref, lnb_ref, ws_ref, bs_ref, a_ref, vn_ref):
    tm = h_ref.shape[0]
    xn = _rms(h_ref[...], g_ref[...]).astype(BF16)
    z = _dot(xn, w_ref[...])
    uv = 0.5 * z * (1.0 + lax.erf(z * np.float32(2.0 ** -0.5)))
    u = uv[:, :GMLP_WIDTH]
    v = uv[:, GMLP_WIDTH:]
    mu = jnp.mean(v, axis=-1, keepdims=True)
    vc = v - mu
    var = jnp.mean(vc * vc, axis=-1, keepdims=True)
    vn = vc * lax.rsqrt(var + EPS) * lng_ref[...] + lnb_ref[...]
    vn_ref[...] = vn
    lane = lax.broadcasted_iota(jnp.int32, (GMLP_BLOCK, LANES), 1)
    low = lane < GMLP_GROUP_DIM
    for r in range(tm // GMLP_BLOCK):
        rows = slice(r * GMLP_BLOCK, (r + 1) * GMLP_BLOCK)
        for p in range(GMLP_WIDTH // LANES):
            cols = slice(p * LANES, (p + 1) * LANES)
            vs = vn[rows, cols].astype(BF16)
            zero = jnp.zeros_like(vs)
            rhs = jnp.concatenate([jnp.where(low, vs, zero), jnp.where(low, zero, vs)], axis=0)
            mixed = _dot(ws_ref[p], rhs) + bs_ref[p]
            a_ref[rows, cols] = (u[rows, cols] * mixed).astype(a_ref.dtype)


def _uv_call(h, g, w_uv, ln_g, ln_b, ws_pairs, bs_pairs, tm):
    n, d = h.shape
    const = lambda *shape: pl.BlockSpec(shape, lambda t: (0,) * len(shape))
    return pl.pallas_call(
        _uv_kernel,
        grid=(n // tm,),
        in_specs=[
            pl.BlockSpec((tm, d), lambda t: (t, 0)),
            const(1, d),
            const(d, 2 * GMLP_WIDTH),
            const(1, GMLP_WIDTH),
            const(1, GMLP_WIDTH),
            const(*ws_pairs.shape),
            const(*bs_pairs.shape),
        ],
        out_specs=[
            pl.BlockSpec((tm, GMLP_WIDTH), lambda t: (t, 0)),
            pl.BlockSpec((tm, GMLP_WIDTH), lambda t: (t, 0)),
        ],
        out_shape=[
            jax.ShapeDtypeStruct((n, GMLP_WIDTH), BF16),
            jax.ShapeDtypeStruct((n, GMLP_WIDTH), F32),
        ],
        compiler_params=_cparams("parallel"),
        name="inproj_gmlp",
    )(h, g, w_uv, ln_g, ln_b, ws_pairs, bs_pairs)


def _qkv_kernel(h_ref, g_ref, w_ref, q_ref, k_ref, v_ref, *bf_refs, for_flash):
    xn = _rms(h_ref[...], g_ref[...]).astype(BF16)
    q = _dot(xn, w_ref[:, :FOX_WIDTH])
    k = _dot(xn, w_ref[:, FOX_WIDTH:2 * FOX_WIDTH])
    v = _dot(xn, w_ref[:, 2 * FOX_WIDTH:])
    k_ref[...] = k
    v_ref[...] = v
    if for_flash:
        kb_ref, vt_ref = bf_refs
        q_ref[...] = (q * np.float32(FOX_HEAD_DIM ** -0.5 * LOG2E)).T.astype(BF16)
        kb_ref[...] = k.astype(BF16)
        vt_ref[...] = v.T.astype(BF16)
    else:
        q_ref[...] = q.astype(BF16)


def _qkv_call(h, g, w_qkv, tm, for_flash):
    n, d = h.shape
    row = lambda width: pl.BlockSpec((tm, width), lambda t: (t, 0))
    col = pl.BlockSpec((FOX_WIDTH, tm), lambda t: (0, t))
    rows_f32 = jax.ShapeDtypeStruct((n, FOX_WIDTH), F32)
    rows_bf16 = jax.ShapeDtypeStruct((n, FOX_WIDTH), BF16)
    cols_bf16 = jax.ShapeDtypeStruct((FOX_WIDTH, n), BF16)
    if for_flash:
        out_specs = [col, row(FOX_WIDTH), row(FOX_WIDTH), row(FOX_WIDTH), col]
        out_shape = [cols_bf16, rows_f32, rows_f32, rows_bf16, cols_bf16]
    else:
        out_specs = [row(FOX_WIDTH)] * 3
        out_shape = [rows_bf16, rows_f32, rows_f32]
    return pl.pallas_call(
        functools.partial(_qkv_kernel, for_flash=for_flash),
        grid=(n // tm,),
        in_specs=[row(d), pl.BlockSpec((1, d), lambda t: (0, 0)),
                  pl.BlockSpec((d, 3 * FOX_WIDTH), lambda t: (0, 0))],
        out_specs=out_specs,
        out_shape=out_shape,
        compiler_params=_cparams("parallel"),
        name="inproj_qkv",
    )(h, g, w_qkv)


def _split3(x):
    hi = x.astype(BF16)
    r = x - hi.astype(F32)
    mid = r.astype(BF16)
    lo = (r - mid.astype(F32)).astype(BF16)
    return hi, mid, lo


def _fmg_kernel(h_ref, g_ref, wf_ref, bf_ref, wmq_ref, wg_ref, mk_ref, mv_ref, *rest,
                tiles_per_seq, bias_cols):
    if bias_cols:
        tri_ref, sel_ref, logf_ref, m_ref, gates_ref, kf_ref, carry_sc = rest
    else:
        logf_ref, m_ref, gates_ref = rest
    tm = h_ref.shape[0]
    xn = _rms(h_ref[...], g_ref[...]).astype(BF16)
    zf = _dot(xn, wf_ref[...]) + bf_ref[...]
    logf = jnp.minimum(zf, 0.0) - jnp.log1p(jnp.exp(-jnp.abs(zf)))
    logf_ref[...] = logf[:, :FOX_HEADS]
    if bias_cols:
        @pl.when(pl.program_id(0) % tiles_per_seq == 0)
        def _():
            carry_sc[...] = jnp.zeros_like(carry_sc)

        lane = lax.broadcasted_iota(jnp.int32, logf.shape, 1)
        x = jnp.where(lane < FOX_HEADS, logf, 0.0)
        tri = tri_ref[...]
        f = sum(_dot(tri, part) for part in _split3(x)) + carry_sc[...]
        carry_sc[...] = f[tm - 1:tm, :]
        kf = sum(_dot(part, sel_ref[c]) for c, part in enumerate(_split3(f * np.float32(-LOG2E))))
        kf_ref[...] = kf.astype(BF16)
    mq = _dot(xn, wmq_ref[...]).astype(BF16)
    for hh in range(MEM_HEADS):
        cols = slice(hh * MEM_HEAD_DIM, (hh + 1) * MEM_HEAD_DIM)
        s = _dot_nt(mq[:, cols], mk_ref[:, cols].astype(BF16)) * np.float32(MEM_HEAD_DIM ** -0.5)
        p = jnp.exp(s - jnp.max(s, axis=-1, keepdims=True))
        p = (p / jnp.sum(p, axis=-1, keepdims=True)).astype(BF16)
        m_ref[:, cols] = _dot(p, mv_ref[:, cols].astype(BF16)).astype(m_ref.dtype)
    gates_ref[...] = _sigmoid(_dot(xn, wg_ref[...])).astype(gates_ref.dtype)


def _fmg_call(h, g, w_f, b_f, w_mq, w_g, mk, mv, batch, tm, gates_dtype, bias_cols):
    n, d = h.shape
    seq = n // batch
    tpb = seq // tm
    n_mem = mk.shape[0] // batch
    const = lambda *shape: pl.BlockSpec(shape, lambda t: (0,) * len(shape))
    row = lambda width: pl.BlockSpec((tm, width), lambda t: (t, 0))
    args = [h, g, w_f, b_f, w_mq, w_g, mk, mv]
    in_specs = [
        row(d), const(1, d), const(d, LANES), const(1, LANES), const(d, MEM_WIDTH),
        const(d, N_BRANCH * d),
        pl.BlockSpec((n_mem, MEM_WIDTH), lambda t: (t // tpb, 0)),
        pl.BlockSpec((n_mem, MEM_WIDTH), lambda t: (t // tpb, 0)),
    ]
    out_specs = [row(FOX_HEADS), row(MEM_WIDTH), row(N_BRANCH * d)]
    out_shape = [
        jax.ShapeDtypeStruct((n, FOX_HEADS), F32),
        jax.ShapeDtypeStruct((n, MEM_WIDTH), BF16),
        jax.ShapeDtypeStruct((n, N_BRANCH * d), gates_dtype),
    ]
    scratch = []
    if bias_cols:
        tri = jnp.tril(jnp.ones((tm, tm), BF16))
        head = jnp.arange(FOX_HEADS)
        sel = jnp.zeros((3, LANES, FOX_WIDTH), BF16)
        for c in range(3):
            sel = sel.at[c, head, head * FOX_HEAD_DIM + c].set(1)
        args += [tri, sel]
        in_specs += [const(tm, tm), const(3, LANES, FOX_WIDTH)]
        out_specs.append(row(FOX_WIDTH))
        out_shape.append(jax.ShapeDtypeStruct((n, FOX_WIDTH), BF16))
        scratch.append(pltpu.VMEM((1, LANES), F32))
    return pl.pallas_call(
        functools.partial(_fmg_kernel, tiles_per_seq=tpb, bias_cols=bias_cols),
        grid=(n // tm,),
        in_specs=in_specs,
        out_specs=out_specs,
        out_shape=out_shape,
        scratch_shapes=scratch,
        compiler_params=_cparams("arbitrary" if bias_cols else "parallel"),
        name="inproj_logf_mem_gates",
    )(*args)


def _memkv_kernel(x_ref, g_ref, w_ref, mk_ref, mv_ref):
    xn = _rms(x_ref[...], g_ref[...]).astype(BF16)
    mk_ref[...] = _dot(xn, w_ref[:, :MEM_WIDTH])
    mv_ref[...] = _dot(xn, w_ref[:, MEM_WIDTH:])


def _memkv_call(mem, g, w_kv, tm):
    n, d = mem.shape
    return pl.pallas_call(
        _memkv_kernel,
        grid=(n // tm,),
        in_specs=[pl.BlockSpec((tm, d), lambda t: (t, 0)),
                  pl.BlockSpec((1, d), lambda t: (0, 0)),
                  pl.BlockSpec((d, 2 * MEM_WIDTH), lambda t: (0, 0))],
        out_specs=[pl.BlockSpec((tm, MEM_WIDTH), lambda t: (t, 0))] * 2,
        out_shape=[jax.ShapeDtypeStruct((n, MEM_WIDTH), F32)] * 2,
        compiler_params=_cparams("parallel"),
        name="memory_kv",
    )(mem, g, w_kv)


def _neg_cumsum_kernel(x_ref, o_ref):
    x = x_ref[...]
    width = x.shape[-1]
    lane = lax.broadcasted_iota(jnp.int32, x.shape, 1)
    d = 1
    while d < width:
        x = x + jnp.where(lane >= d, pltpu.roll(x, d, 1), 0.0)
        d *= 2
    o_ref[...] = -x


def _neg_cumsum_call(x):
    return pl.pallas_call(
        _neg_cumsum_kernel,
        out_shape=jax.ShapeDtypeStruct(x.shape, F32),
        compiler_params=pltpu.CompilerParams(vmem_limit_bytes=VMEM_LIMIT),
        name="logf_cumsum",
    )(x)


def _fox_kernel(qi_ref, kj_ref, flag_ref, qt_ref, ones_ref, k_ref, kf_ref, vt_ref, o_ref,
                m_sc, acc_sc, *, tile, group, heads):
    step = pl.program_id(2)
    flags = flag_ref[step]
    acc_rows = FOX_HEAD_DIM + BF16_ROWS

    @pl.when(flags & 1 != 0)
    def _():
        m_sc[...] = jnp.full_like(m_sc, -jnp.inf)
        acc_sc[...] = jnp.zeros_like(acc_sc)

    def update(diagonal):
        n_groups = tile // group
        scores = []
        for hd in range(heads):
            hs = slice(hd * FOX_HEAD_DIM, (hd + 1) * FOX_HEAD_DIM)
            ka = jnp.concatenate([k_ref[:, hs], kf_ref[:, hs]], axis=1)
            for gi in range(n_groups):
                cols = slice(gi * group, (gi + 1) * group)
                rows = (gi + 1) * group if diagonal else tile
                qa = jnp.concatenate([qt_ref[hs, cols], ones_ref[:, :group]], axis=0)
                s = _dot(ka[:rows], qa)
                if diagonal:
                    key = lax.broadcasted_iota(jnp.int32, s.shape, 0)
                    qry = lax.broadcasted_iota(jnp.int32, s.shape, 1) + gi * group
                    s = jnp.where(key <= qry, s, -jnp.inf)
                scores.append(s)
        for hd in range(heads):
            hs = slice(hd * FOX_HEAD_DIM, (hd + 1) * FOX_HEAD_DIM)
            ar = slice(hd * acc_rows, (hd + 1) * acc_rows)
            va = jnp.concatenate([vt_ref[hs, :], ones_ref[:BF16_ROWS]], axis=0)
            for gi in range(n_groups):
                s = scores[hd * n_groups + gi]
                cols = slice(gi * group, (gi + 1) * group)
                rows = s.shape[0]
                m_prev = m_sc[hd:hd + 1, cols]
                m_new = jnp.maximum(m_prev, jnp.max(s, axis=0, keepdims=True))
                p = jnp.exp2(s - m_new).astype(BF16)
                acc_sc[ar, cols] = (jnp.exp2(m_prev - m_new) * acc_sc[ar, cols]
                                    + _dot(va[:, :rows], p))
                m_sc[hd:hd + 1, cols] = m_new

    @pl.when(flags & 2 == 0)
    def _():
        update(False)

    @pl.when(flags & 2 != 0)
    def _():
        update(True)
        for hd in range(heads):
            acc = acc_sc[hd * acc_rows:(hd + 1) * acc_rows]
            out = acc[:FOX_HEAD_DIM] * (1.0 / acc[FOX_HEAD_DIM:FOX_HEAD_DIM + 1])
            o_ref[:, hd * FOX_HEAD_DIM:(hd + 1) * FOX_HEAD_DIM] = out.T.astype(o_ref.dtype)


def _fox_tables(n_tiles):
    qi, kj, flags = [], [], []
    for i in range(n_tiles):
        for j in range(i + 1):
            qi.append(i)
            kj.append(j)
            flags.append((1 if j == 0 else 0) | (2 if j == i else 0))
    return (np.asarray(qi, np.int32), np.asarray(kj, np.int32), np.asarray(flags, np.int32))


def _fox_call(q_t, k, kf, v_t, batch, tile, group, heads_per_step):
    width, n = q_t.shape
    seq = n // batch
    hw = heads_per_step * FOX_HEAD_DIM
    nt = seq // tile
    qi, kj, flags = _fox_tables(nt)
    ones = jnp.zeros((FOX_HEAD_DIM, tile), BF16).at[:3].set(1)
    q_map = lambda b, h, s, qi, kj, fl: (h, b * nt + qi[s])
    kt_map = lambda b, h, s, qi, kj, fl: (h, b * nt + kj[s])
    k_map = lambda b, h, s, qi, kj, fl: (b * nt + kj[s], h)
    grid_spec = pltpu.PrefetchScalarGridSpec(
        num_scalar_prefetch=3,
        grid=(batch, width // hw, len(qi)),
        in_specs=[
            pl.BlockSpec((hw, tile), q_map),
            pl.BlockSpec((FOX_HEAD_DIM, tile), lambda b, h, s, qi, kj, fl: (0, 0)),
            pl.BlockSpec((tile, hw), k_map),
            pl.BlockSpec((tile, hw), k_map),
            pl.BlockSpec((hw, tile), kt_map),
        ],
        out_specs=pl.BlockSpec((tile, hw), lambda b, h, s, qi, kj, fl: (b * nt + qi[s], h)),
        scratch_shapes=[
            pltpu.VMEM((heads_per_step, tile), F32),
            pltpu.VMEM((heads_per_step * (FOX_HEAD_DIM + BF16_ROWS), tile), F32),
        ],
    )
    return pl.pallas_call(
        functools.partial(_fox_kernel, tile=tile, group=group, heads=heads_per_step),
        grid_spec=grid_spec,
        out_shape=jax.ShapeDtypeStruct((n, width), BF16),
        compiler_params=_cparams("parallel", "parallel", "arbitrary"),
        name="fox_attention",
    )(jnp.asarray(qi), jnp.asarray(kj), jnp.asarray(flags), q_t, ones, k, kf, v_t)


def _fox_sample_kernel(q_ref, kc_ref, vc_ref, kn_ref, vn_ref, nf_ref, o_ref):
    t = q_ref.shape[0]
    past = kc_ref.shape[0]
    q = q_ref[...]
    nf = nf_ref[0]
    pad = jnp.zeros((LANES - t, FOX_HEAD_DIM), BF16)
    kn = jnp.concatenate([kn_ref[...].astype(BF16), pad], axis=0)
    vn = jnp.concatenate([vn_ref[...].astype(BF16), pad], axis=0)
    scale = np.float32(FOX_HEAD_DIM ** -0.5)
    s_c = _dot_nt(q, kc_ref[...].astype(BF16)) * scale + nf[:, :past]
    s_n = _dot_nt(q, kn) * scale + nf[:, past:]
    row = lax.broadcasted_iota(jnp.int32, (t, LANES), 0)
    col = lax.broadcasted_iota(jnp.int32, (t, LANES), 1)
    s_n = jnp.where(col <= row, s_n, -jnp.inf)
    m = jnp.maximum(jnp.max(s_c, axis=-1, keepdims=True), jnp.max(s_n, axis=-1, keepdims=True))
    p_c = jnp.exp(s_c - m)
    p_n = jnp.exp(s_n - m)
    l = jnp.sum(p_c, axis=-1, keepdims=True) + jnp.sum(p_n, axis=-1, keepdims=True)
    p_c = (p_c / l).astype(BF16)
    p_n = (p_n / l).astype(BF16)
    o_ref[...] = (_dot(p_c, vc_ref[...].astype(BF16)) + _dot(p_n, vn)).astype(o_ref.dtype)


def _fox_sample_call(q, k_cache, v_cache, layer, k_new, v_new, neg_f, batch, past):
    n, width = q.shape
    t = n // batch
    heads = width // FOX_HEAD_DIM
    nf = neg_f.reshape(batch * heads, 1, past + LANES)
    new = pl.BlockSpec((t, FOX_HEAD_DIM), lambda b, h: (b, h))
    old = pl.BlockSpec((past, FOX_HEAD_DIM), lambda b, h: (layer * batch + b, h))
    return pl.pallas_call(
        _fox_sample_kernel,
        grid=(batch, heads),
        in_specs=[new, old, old, new, new,
                  pl.BlockSpec((1, 1, past + LANES), lambda b, h: (b * heads + h, 0, 0))],
        out_specs=new,
        out_shape=jax.ShapeDtypeStruct((n, width), BF16),
        compiler_params=_cparams("parallel", "parallel"),
        name="fox_attention_sample",
    )(q, k_cache, v_cache, k_new, v_new, nf)


def _merge_kernel(h_ref, a_ref, b_ref, m_ref, gates_ref, wpa_ref, wpb_ref, wpm_ref, wo_ref, o_ref):
    d = h_ref.shape[1]
    y = gates_ref[:, :d].astype(F32) * _dot(a_ref[...], wpa_ref[...])
    y = y + gates_ref[:, d:2 * d].astype(F32) * _dot(b_ref[...], wpb_ref[...])
    y = y + gates_ref[:, 2 * d:].astype(F32) * _dot(m_ref[...], wpm_ref[...])
    o_ref[...] = h_ref[...] + _dot(y.astype(BF16), wo_ref[...])


def _merge_call(h, a, b, m, gates, w_pa, w_pb, w_pm, w_o, tm):
    n, d = h.shape
    row = lambda width: pl.BlockSpec((tm, width), lambda t: (t, 0))
    const = lambda w: pl.BlockSpec(w.shape, lambda t: (0, 0))
    return pl.pallas_call(
        _merge_kernel,
        grid=(n // tm,),
        in_specs=[row(d), row(GMLP_WIDTH), row(FOX_WIDTH), row(MEM_WIDTH), row(N_BRANCH * d),
                  const(w_pa), const(w_pb), const(w_pm), const(w_o)],
        out_specs=row(d),
        out_shape=jax.ShapeDtypeStruct((n, d), F32),
        compiler_params=_cparams("parallel"),
        name="merge_branches",
    )(h, a, b, m, gates, w_pa, w_pb, w_pm, w_o)


def _top2(logits, n_exp):
    lane = lax.broadcasted_iota(jnp.int32, logits.shape, 1)
    neg = jnp.float32(-jnp.inf)
    logits = jnp.where(lane < n_exp, logits, neg)
    v1 = jnp.max(logits, axis=-1, keepdims=True)
    i1 = jnp.min(jnp.where(logits == v1, lane, LANES), axis=-1, keepdims=True)
    rest = jnp.where(lane == i1, neg, logits)
    v2 = jnp.max(rest, axis=-1, keepdims=True)
    i2 = jnp.min(jnp.where(rest == v2, lane, LANES), axis=-1, keepdims=True)
    e2 = jnp.exp(v2 - v1)
    return lane, i1, i2, 1.0 / (1.0 + e2), e2 / (1.0 + e2)


def _top2_combine(logits, n_exp):
    lane, i1, i2, p1, p2 = _top2(logits, n_exp)
    return jnp.where(lane == i1, p1, 0.0) + jnp.where(lane == i2, p2, 0.0)


def _mixer_kernel(h_ref, g_ref, *refs, routed, final_norm):
    refs = list(refs)
    wr_ref = refs.pop(0) if routed else None
    gfin_ref = refs.pop(0) if final_norm else None
    w1_ref, w3_ref, w2_ref, o_ref, xn_sc, acc_sc = refs[:6]
    comb_sc = refs[6] if routed else None
    e = pl.program_id(1)
    c = pl.program_id(2)
    first = jnp.logical_and(e == 0, c == 0)
    last = jnp.logical_and(e == pl.num_programs(1) - 1, c == pl.num_programs(2) - 1)

    @pl.when(first)
    def _():
        xn = _rms(h_ref[...], g_ref[...])
        xn_sc[...] = xn.astype(BF16)
        acc_sc[...] = jnp.zeros_like(acc_sc)
        if routed:
            logits = _dot(xn.astype(BF16), wr_ref[...])
            comb_sc[...] = _top2_combine(logits, pl.num_programs(1))

    xb = xn_sc[...]
    g1 = _dot(xb, w1_ref[0])
    g3 = _dot(xb, w3_ref[0])
    mid = (g1 * _sigmoid(g1) * g3).astype(BF16)
    y = _dot(mid, w2_ref[0])
    if routed:
        lane = lax.broadcasted_iota(jnp.int32, comb_sc.shape, 1)
        y = y * jnp.sum(jnp.where(lane == e, comb_sc[...], 0.0), axis=-1, keepdims=True)
    acc_sc[...] += y

    @pl.when(last)
    def _():
        out = h_ref[...] + acc_sc[...]
        if final_norm:
            out = _rms(out, gfin_ref[...])
        o_ref[...] = out


def _mixer_call(h, g, w1, w3, w2, tm, tf, w_router=None, g_final=None):
    n, d = h.shape
    n_exp, _, ff = w1.shape
    routed = w_router is not None
    final_norm = g_final is not None
    row = pl.BlockSpec((tm, d), lambda t, e, c: (t, 0))
    vec = pl.BlockSpec((1, d), lambda t, e, c: (0, 0))
    args, in_specs = [h, g], [row, vec]
    if routed:
        args.append(w_router)
        in_specs.append(pl.BlockSpec(w_router.shape, lambda t, e, c: (0, 0)))
    if final_norm:
        args.append(g_final)
        in_specs.append(vec)
    args += [w1, w3, w2]
    in_specs += [
        pl.BlockSpec((1, d, tf), lambda t, e, c: (e, 0, c)),
        pl.BlockSpec((1, d, tf), lambda t, e, c: (e, 0, c)),
        pl.BlockSpec((1, tf, d), lambda t, e, c: (e, c, 0)),
    ]
    scratch = [pltpu.VMEM((tm, d), BF16), pltpu.VMEM((tm, d), F32)]
    if routed:
        scratch.append(pltpu.VMEM((tm, LANES), F32))
    return pl.pallas_call(
        functools.partial(_mixer_kernel, routed=routed, final_norm=final_norm),
        grid=(n // tm, n_exp, ff // tf),
        in_specs=in_specs,
        out_specs=row,
        out_shape=jax.ShapeDtypeStruct((n, d), F32),
        scratch_shapes=scratch,
        compiler_params=_cparams("parallel", "arbitrary", "arbitrary"),
        name="channel_mixer",
    )(*args)


def _router_kernel(h_ref, g_ref, wr_ref, idx_ref, p1_ref, p2_ref, *, n_exp):
    xn = _rms(h_ref[...], g_ref[...]).astype(BF16)
    lane, i1, i2, p1, p2 = _top2(_dot(xn, wr_ref[...]), n_exp)
    idx_ref[...] = jnp.where(lane == 0, i1, jnp.where(lane == 1, i2, 0))
    p1_ref[...] = jnp.broadcast_to(p1, p1_ref.shape)
    p2_ref[...] = jnp.broadcast_to(p2, p2_ref.shape)


def _router_call(h, g, w_router, n_exp, tm):
    n, d = h.shape
    wide = pl.BlockSpec((tm, LANES), lambda t: (t, 0))
    return pl.pallas_call(
        functools.partial(_router_kernel, n_exp=n_exp),
        grid=(n // tm,),
        in_specs=[pl.BlockSpec((tm, d), lambda t: (t, 0)), pl.BlockSpec((1, d), lambda t: (0, 0)),
                  pl.BlockSpec((d, LANES), lambda t: (0, 0))],
        out_specs=[wide, wide, wide],
        out_shape=[jax.ShapeDtypeStruct((n, LANES), jnp.int32),
                   jax.ShapeDtypeStruct((n, LANES), F32),
                   jax.ShapeDtypeStruct((n, LANES), F32)],
        compiler_params=_cparams("parallel"),
        name="moe_router",
    )(h, g, w_router)


def _row_copy(src_ref, src_row, dst_ref, dst_row, sem):
    return pltpu.make_async_copy(src_ref.at[pl.ds(src_row, 1)], dst_ref.at[pl.ds(dst_row, 1)], sem)


def _dispatch_kernel(pos1_ref, pos2_ref, h_ref, g_ref, zero_ref, xs_ref, x_sc, sem):
    del zero_ref
    tm = h_ref.shape[0]
    base = pl.program_id(0) * tm
    x_sc[...] = _rms(h_ref[...], g_ref[...])

    def start(i, carry):
        _row_copy(x_sc, i, xs_ref, pos1_ref[base + i], sem).start(priority=0)
        _row_copy(x_sc, i, xs_ref, pos2_ref[base + i], sem).start(priority=1)
        return carry

    def wait(i, carry):
        _row_copy(x_sc, i, xs_ref, pos1_ref[base + i], sem).wait()
        _row_copy(x_sc, i, xs_ref, pos2_ref[base + i], sem).wait()
        return carry

    lax.fori_loop(0, tm, start, 0)
    lax.fori_loop(0, tm, wait, 0)


def _dispatch_call(h, g, pos1, pos2, n_rows, tm):
    n, d = h.shape
    grid_spec = pltpu.PrefetchScalarGridSpec(
        num_scalar_prefetch=2,
        grid=(n // tm,),
        in_specs=[pl.BlockSpec((tm, d), lambda t, p1, p2: (t, 0)),
                  pl.BlockSpec((1, d), lambda t, p1, p2: (0, 0)),
                  pl.BlockSpec(memory_space=pl.ANY)],
        out_specs=pl.BlockSpec(memory_space=pl.ANY),
        scratch_shapes=[pltpu.VMEM((tm, d), F32), pltpu.SemaphoreType.DMA(())],
    )
    return pl.pallas_call(
        _dispatch_kernel,
        grid_spec=grid_spec,
        out_shape=jax.ShapeDtypeStruct((n_rows, d), F32),
        input_output_aliases={4: 0},
        compiler_params=_cparams("arbitrary"),
        name="moe_dispatch",
    )(pos1, pos2, h, g, jnp.zeros((n_rows, d), F32))


def _experts_kernel(te_ref, nact_ref, x_ref, w1_ref, w3_ref, w2_ref, y_ref, xb_sc, acc_sc):
    t = pl.program_id(0)
    c = pl.program_id(1)

    @pl.when(c == 0)
    def _():
        xb_sc[...] = x_ref[...].astype(BF16)
        acc_sc[...] = jnp.zeros_like(acc_sc)

    @pl.when(t < nact_ref[0])
    def _():
        xb = xb_sc[...]
        g1 = _dot(xb, w1_ref[0])
        g3 = _dot(xb, w3_ref[0])
        mid = (g1 * _sigmoid(g1) * g3).astype(BF16)
        acc_sc[...] += _dot(mid, w2_ref[0])

    @pl.when(c == pl.num_programs(1) - 1)
    def _():
        y_ref[...] = acc_sc[...]


def _experts_call(xs, tile_expert, n_active, w1, w3, w2, tm, tf):
    n_rows, d = xs.shape
    ff = w1.shape[2]
    nff = ff // tf

    def chunk(t, c, te, nact):
        return jnp.where(t < nact[0], c, nff - 1)

    grid_spec = pltpu.PrefetchScalarGridSpec(
        num_scalar_prefetch=2,
        grid=(n_rows // tm, nff),
        in_specs=[
            pl.BlockSpec((tm, d), lambda t, c, te, nact: (t, 0)),
            pl.BlockSpec((1, d, tf), lambda t, c, te, nact: (te[t], 0, chunk(t, c, te, nact))),
            pl.BlockSpec((1, d, tf), lambda t, c, te, nact: (te[t], 0, chunk(t, c, te, nact))),
            pl.BlockSpec((1, tf, d), lambda t, c, te, nact: (te[t], chunk(t, c, te, nact), 0)),
        ],
        out_specs=pl.BlockSpec((tm, d), lambda t, c, te, nact: (t, 0)),
        scratch_shapes=[pltpu.VMEM((tm, d), BF16), pltpu.VMEM((tm, d), F32)],
    )
    return pl.pallas_call(
        _experts_kernel,
        grid_spec=grid_spec,
        out_shape=jax.ShapeDtypeStruct((n_rows, d), F32),
        compiler_params=_cparams("parallel", "arbitrary"),
        name="moe_experts",
    )(tile_expert, n_active, xs, w1, w3, w2)


def _combine_kernel(pos1_ref, pos2_ref, h_ref, p1_ref, p2_ref, *refs, final_norm):
    refs = list(refs)
    gfin_ref = refs.pop(0) if final_norm else None
    ys_ref, o_ref, y1_sc, y2_sc, sem = refs
    tm, d = h_ref.shape
    base = pl.program_id(0) * tm

    def start(i, carry):
        _row_copy(ys_ref, pos1_ref[base + i], y1_sc, i, sem).start(priority=0)
        _row_copy(ys_ref, pos2_ref[base + i], y2_sc, i, sem).start(priority=1)
        return carry

    def wait(i, carry):
        _row_copy(ys_ref, pos1_ref[base + i], y1_sc, i, sem).wait()
        _row_copy(ys_ref, pos2_ref[base + i], y2_sc, i, sem).wait()
        return carry

    lax.fori_loop(0, tm, start, 0)
    lax.fori_loop(0, tm, wait, 0)
    w1 = jnp.concatenate([p1_ref[...]] * (d // LANES), axis=1)
    w2 = jnp.concatenate([p2_ref[...]] * (d // LANES), axis=1)
    out = h_ref[...] + (w1 * y1_sc[...] + w2 * y2_sc[...])
    if final_norm:
        out = _rms(out, gfin_ref[...])
    o_ref[...] = out


def _combine_call(h, p1, p2, ys, pos1, pos2, tm, g_final=None):
    n, d = h.shape
    final_norm = g_final is not None
    row = pl.BlockSpec((tm, d), lambda t, a, b: (t, 0))
    wide = pl.BlockSpec((tm, LANES), lambda t, a, b: (t, 0))
    args, in_specs = [h, p1, p2], [row, wide, wide]
    if final_norm:
        args.append(g_final)
        in_specs.append(pl.BlockSpec((1, d), lambda t, a, b: (0, 0)))
    args.append(ys)
    in_specs.append(pl.BlockSpec(memory_space=pl.ANY))
    grid_spec = pltpu.PrefetchScalarGridSpec(
        num_scalar_prefetch=2,
        grid=(n // tm,),
        in_specs=in_specs,
        out_specs=row,
        scratch_shapes=[pltpu.VMEM((tm, d), F32), pltpu.VMEM((tm, d), F32),
                        pltpu.SemaphoreType.DMA(())],
    )
    return pl.pallas_call(
        functools.partial(_combine_kernel, final_norm=final_norm),
        grid_spec=grid_spec,
        out_shape=jax.ShapeDtypeStruct((n, d), F32),
        compiler_params=_cparams("arbitrary"),
        name="moe_combine",
    )(pos1, pos2, *args)


def _routing_tables(idx, n_exp, tm):
    n = idx.shape[0]
    e = idx[:, :TOP_K].reshape(-1)
    onehot = (e[:, None] == jnp.arange(n_exp, dtype=jnp.int32)[None, :]).astype(jnp.int32)
    csum = jnp.cumsum(onehot, axis=0)
    rank = jnp.take_along_axis(csum, e[:, None], axis=1)[:, 0] - 1
    counts = csum[-1]
    padded = (counts + tm - 1) // tm * tm
    seg_end = jnp.cumsum(padded)
    pos = ((seg_end - padded)[e] + rank).reshape(n, TOP_K)
    n_rows = TOP_K * n + n_exp * tm
    tile_first = jnp.arange(n_rows // tm, dtype=jnp.int32) * tm
    tile_expert = jnp.minimum(jnp.sum(tile_first[:, None] >= seg_end[None, :], axis=1), n_exp - 1)
    n_active = (seg_end[-1] // tm).reshape(1)
    return pos[:, 0], pos[:, 1], tile_expert.astype(jnp.int32), n_active.astype(jnp.int32), n_rows


def _routed_moe(h, g, w_router, w1, w3, w2, tm_tok, tm_rows, tf, g_final):
    n_exp = w1.shape[0]
    idx, p1, p2 = _router_call(h, g, w_router, n_exp, tm_tok)
    pos1, pos2, tile_expert, n_active, n_rows = _routing_tables(idx, n_exp, tm_rows)
    xs = _dispatch_call(h, g, pos1, pos2, n_rows, tm_tok)
    ys = _experts_call(xs, tile_expert, n_active, w1, w3, w2, tm_rows, tf)
    return _combine_call(h, p1, p2, ys, pos1, pos2, tm_tok, g_final)


def _largest_tile(n, cap, quantum):
    best = quantum
    t = quantum
    while t <= min(n, cap):
        if n % t == 0:
            best = t
        t += quantum
    return best


def _spatial_operands(w_s, b_s, blk):
    reps = GMLP_BLOCK // blk
    w = w_s[:, :blk, :blk] * jnp.tril(jnp.ones((blk, blk), w_s.dtype))
    eye = jnp.eye(reps, dtype=w_s.dtype)
    w = jnp.einsum("ab,gts->gatbs", eye, w).reshape(GMLP_GROUPS, GMLP_BLOCK, GMLP_BLOCK)
    b = jnp.tile(b_s[:, :blk], (1, reps))
    ws_pairs = jnp.concatenate([w[0::2], w[1::2]], axis=2).astype(BF16)
    bs_pairs = jnp.concatenate(
        [jnp.broadcast_to(b[0::2, :, None], (GMLP_GROUPS // 2, GMLP_BLOCK, GMLP_GROUP_DIM)),
         jnp.broadcast_to(b[1::2, :, None], (GMLP_GROUPS // 2, GMLP_BLOCK, GMLP_GROUP_DIM))],
        axis=2).astype(F32)
    return ws_pairs, bs_pairs


def kernel(x_prompt, x_sample, cache_fox_k, cache_fox_v, cache_fox_logf, cache_mem_k, cache_mem_v, mem_prompt, g_mix, w_in, b_f, ln_v_g, ln_v_b, w_s, b_s, g_mem, w_mem_kv, w_pa, w_pb, w_pm, w_o, g_ffn, w1, w3, w2, w_router, e_w1, e_w3, e_w2, g_final):
    depth = w_in.shape[0]
    bp, sp, d = x_prompt.shape
    bs, ts, _ = x_sample.shape
    past = cache_fox_k.shape[2]
    n_mem = mem_prompt.shape[1]
    n_exp = e_w1.shape[1]
    assert sp % GMLP_BLOCK == 0 and GMLP_BLOCK % ts == 0 and (bs * ts) % GMLP_BLOCK == 0

    off_q = 2 * GMLP_WIDTH
    off_f = off_q + 3 * FOX_WIDTH
    off_mq = off_f + FOX_HEADS
    off_g = off_mq + MEM_WIDTH

    hp = x_prompt.reshape(bp * sp, d)
    hs = x_sample.reshape(bs * ts, d)
    mem = mem_prompt.reshape(bp * n_mem, d)
    cache_k_rows = cache_fox_k.reshape(depth * bs * past, FOX_WIDTH)
    cache_v_rows = cache_fox_v.reshape(depth * bs * past, FOX_WIDTH)

    tm_p = _largest_tile(sp, 512, GMLP_BLOCK)
    tm_s = GMLP_BLOCK
    fox_group = 2 * LANES
    fox_heads = 4
    fox_tile = _largest_tile(sp, 1024, fox_group)
    tmix_p = _largest_tile(bp * sp, 1024, LANES)
    tmix_s = bs * ts

    outs = {k: [] for k in ("fk_p", "fv_p", "fl_p", "mk_p", "mv_p", "fk_s", "fv_s", "fl_s", "gv_s")}
    for i in range(depth):
        g = g_mix[i].reshape(1, d)
        wi = w_in[i]
        w_uv = wi[:, :off_q].astype(BF16)
        w_qkv = wi[:, off_q:off_f].astype(BF16)
        w_f = jnp.zeros((d, LANES), BF16).at[:, :FOX_HEADS].set(wi[:, off_f:off_mq].astype(BF16))
        w_mq = wi[:, off_mq:off_g].astype(BF16)
        w_g = wi[:, off_g:].astype(BF16)
        bfi = jnp.zeros((1, LANES), F32).at[0, :FOX_HEADS].set(b_f[i])
        lng = ln_v_g[i].reshape(1, GMLP_WIDTH)
        lnb = ln_v_b[i].reshape(1, GMLP_WIDTH)
        wpa, wpb, wpm, wo = (w[i].astype(BF16) for w in (w_pa, w_pb, w_pm, w_o))
        gf = g_ffn[i].reshape(1, d)
        j = i // 2
        if i % 2 == 0:
            mw1, mw3, mw2 = (w[j][None].astype(BF16) for w in (w1, w3, w2))
            wr = None
        else:
            mw1, mw3, mw2 = (w[j].astype(BF16) for w in (e_w1, e_w3, e_w2))
            wr = jnp.zeros((d, LANES), BF16).at[:, :n_exp].set(w_router[j].astype(BF16))
        ff = mw1.shape[2]
        tf = _largest_tile(ff, 1408, LANES)
        gfin = g_final.reshape(1, d) if i == depth - 1 else None

        ws_pairs, bs_pairs = _spatial_operands(w_s[i], b_s[i], GMLP_BLOCK)
        mk, mv = _memkv_call(mem, g_mem[i].reshape(1, d), w_mem_kv[i].astype(BF16), n_mem)
        a, _ = _uv_call(hp, g, w_uv, lng, lnb, ws_pairs, bs_pairs, tm_p)
        q_t, k, v, kb, v_t = _qkv_call(hp, g, w_qkv, tm_p, True)
        logf, m, gates, kf = _fmg_call(hp, g, w_f, bfi, w_mq, w_g, mk, mv, bp, tm_p, BF16, True)
        b = _fox_call(q_t, kb, kf, v_t, bp, fox_tile, fox_group, fox_heads)
        hp = _merge_call(hp, a, b, m, gates, wpa, wpb, wpm, wo, tm_p)
        if wr is None:
            hp = _mixer_call(hp, gf, mw1, mw3, mw2, tmix_p, tf, None, gfin)
        else:
            hp = _routed_moe(hp, gf, wr, mw1, mw3, mw2, tm_p, tmix_p, tf, gfin)
        outs["fk_p"].append(k.reshape(bp, sp, FOX_HEADS, FOX_HEAD_DIM))
        outs["fv_p"].append(v.reshape(bp, sp, FOX_HEADS, FOX_HEAD_DIM))
        outs["fl_p"].append(logf.reshape(bp, sp, FOX_HEADS))
        outs["mk_p"].append(mk.reshape(bp, n_mem, MEM_HEADS, MEM_HEAD_DIM))
        outs["mv_p"].append(mv.reshape(bp, n_mem, MEM_HEADS, MEM_HEAD_DIM))

        ws_pairs, bs_pairs = _spatial_operands(w_s[i], b_s[i], ts)
        a, vn = _uv_call(hs, g, w_uv, lng, lnb, ws_pairs, bs_pairs, tm_s)
        q, k, v = _qkv_call(hs, g, w_qkv, bs * ts, False)
        logf, m, gates = _fmg_call(hs, g, w_f, bfi, w_mq, w_g,
                                   cache_mem_k[i].reshape(bs * n_mem, MEM_WIDTH),
                                   cache_mem_v[i].reshape(bs * n_mem, MEM_WIDTH), bs, ts, F32, False)
        logf = logf.reshape(bs, ts, FOX_HEADS)
        logf_all = jnp.concatenate(
            [cache_fox_logf[i].transpose(0, 2, 1), logf.transpose(0, 2, 1),
             jnp.zeros((bs, FOX_HEADS, LANES - ts), F32)], axis=2)
        neg_f = _neg_cumsum_call(logf_all.reshape(bs * FOX_HEADS, past + LANES))
        b = _fox_sample_call(q, cache_k_rows, cache_v_rows, i, k, v, neg_f, bs, past)
        hs = _merge_call(hs, a, b, m, gates, wpa, wpb, wpm, wo, tmix_s)
        hs = _mixer_call(hs, gf, mw1, mw3, mw2, tmix_s, tf, wr, gfin)
        outs["fk_s"].append(k.reshape(bs, ts, FOX_HEADS, FOX_HEAD_DIM))
        outs["fv_s"].append(v.reshape(bs, ts, FOX_HEADS, FOX_HEAD_DIM))
        outs["fl_s"].append(logf)
        outs["gv_s"].append(vn.reshape(bs, ts, GMLP_WIDTH))

    st = {k: jnp.stack(v) for k, v in outs.items()}
    return (hp.reshape(bp, sp, d), hs.reshape(bs, ts, d),
            st["fk_p"], st["fv_p"], st["fl_p"], st["mk_p"], st["mv_p"],
            st["fk_s"], st["fv_s"], st["fl_s"], st["gv_s"])
```

```python
import functools

import numpy as np
import jax
import jax.numpy as jnp
from jax import lax
from jax.experimental import pallas as pl
from jax.experimental.pallas import tpu as pltpu

F32 = jnp.float32
BF16 = jnp.bfloat16

EPS = 1e-6
GMLP_GROUPS = 8
GMLP_GROUP_DIM = 64
GMLP_WIDTH = GMLP_GROUPS * GMLP_GROUP_DIM
GMLP_BLOCK = 128
FOX_HEADS = 8
FOX_HEAD_DIM = 128
FOX_WIDTH = FOX_HEADS * FOX_HEAD_DIM
MEM_HEADS = 4
MEM_HEAD_DIM = 128
MEM_WIDTH = MEM_HEADS * MEM_HEAD_DIM
N_BRANCH = 3
TOP_K = 2
LANES = 128
SUBLANES = 8
BF16_ROWS = 16
LOG2E = 1.4426950408889634
VMEM_LIMIT = 56 * 1024 * 1024

_NT = (((1,), (1,)), ((), ()))


def _cparams(*sem):
    return pltpu.CompilerParams(dimension_semantics=sem, vmem_limit_bytes=VMEM_LIMIT)


def _rms(x, g):
    return x * lax.rsqrt(jnp.mean(x * x, axis=-1, keepdims=True) + EPS) * g


def _sigmoid(x):
    return 1.0 / (1.0 + jnp.exp(-x))


def _dot(a, b):
    return jnp.dot(a, b, preferred_element_type=F32)


def _dot_nt(a, b):
    return lax.dot_general(a, b, _NT, preferred_element_type=F32)


def _uv_kernel(h_ref, g_ref, w_ref, lng_ref, lnb_ref, ws_ref, bs_ref, a_ref, vn_ref):
    tm = h_ref.shape[0]
    xn = _rms(h_ref[...], g_ref[...]).astype(BF16)
    z = _dot(xn, w_ref[...])
    uv = 0.5 * z * (1.0 + lax.erf(z * np.float32(2.0 ** -0.5)))
    u = uv[:, :GMLP_WIDTH]
    v = uv[:, GMLP_WIDTH:]
    mu = jnp.mean(v, axis=-1, keepdims=True)
    vc = v - mu
    var = jnp.mean(vc * vc, axis=-1, keepdims=True)
    vn = vc * lax.rsqrt(var + EPS) * lng_ref[...] + lnb_ref[...]
    vn_ref[...] = vn
    lane = lax.broadcasted_iota(jnp.int32, (GMLP_BLOCK, LANES), 1)
    low = lane < GMLP_GROUP_DIM
    for r in range(tm // GMLP_BLOCK):
        rows = slice(r * GMLP_BLOCK, (r + 1) * GMLP_BLOCK)
        for p in range(GMLP_WIDTH // LANES):
            cols = slice(p * LANES, (p + 1) * LANES)
            vs = vn[rows, cols].astype(BF16)
            zero = jnp.zeros_like(vs)
            rhs = jnp.concatenate([jnp.where(low, vs, zero), jnp.where(low, zero, vs)], axis=0)
            mixed = _dot(ws_ref[p], rhs) + bs_ref[p]
            a_ref[rows, cols] = (u[rows, cols] * mixed).astype(a_ref.dtype)


def _uv_call(h, g, w_uv, ln_g, ln_b, ws_pairs, bs_pairs, tm):
    n, d = h.shape
    const = lambda *shape: pl.BlockSpec(shape, lambda t: (0,) * len(shape))
    return pl.pallas_call(
        _uv_kernel,
        grid=(n // tm,),
        in_specs=[
            pl.BlockSpec((tm, d), lambda t: (t, 0)),
            const(1, d),
            const(d, 2 * GMLP_WIDTH),
            const(1, GMLP_WIDTH),
            const(1, GMLP_WIDTH),
            const(*ws_pairs.shape),
            const(*bs_pairs.shape),
        ],
        out_specs=[
            pl.BlockSpec((tm, GMLP_WIDTH), lambda t: (t, 0)),
            pl.BlockSpec((tm, GMLP_WIDTH), lambda t: (t, 0)),
        ],
        out_shape=[
            jax.ShapeDtypeStruct((n, GMLP_WIDTH), BF16),
            jax.ShapeDtypeStruct((n, GMLP_WIDTH), F32),
        ],
        compiler_params=_cparams("parallel"),
        name="inproj_gmlp",
    )(h, g, w_uv, ln_g, ln_b, ws_pairs, bs_pairs)


def _qkv_kernel(h_ref, g_ref, w_ref, q_ref, k_ref, v_ref, *bf_refs, for_flash):
    xn = _rms(h_ref[...], g_ref[...]).astype(BF16)
    q = _dot(xn, w_ref[:, :FOX_WIDTH])
    k = _dot(xn, w_ref[:, FOX_WIDTH:2 * FOX_WIDTH])
    v = _dot(xn, w_ref[:, 2 * FOX_WIDTH:])
    k_ref[...] = k
    v_ref[...] = v
    if for_flash:
        kb_ref, vt_ref = bf_refs
        q_ref[...] = (q * np.float32(FOX_HEAD_DIM ** -0.5 * LOG2E)).T.astype(BF16)
        kb_ref[...] = k.astype(BF16)
        vt_ref[...] = v.T.astype(BF16)
    else:
        q_ref[...] = q.astype(BF16)


def _qkv_call(h, g, w_qkv, tm, for_flash):
    n, d = h.shape
    row = lambda width: pl.BlockSpec((tm, width), lambda t: (t, 0))
    col = pl.BlockSpec((FOX_WIDTH, tm), lambda t: (0, t))
    rows_f32 = jax.ShapeDtypeStruct((n, FOX_WIDTH), F32)
    rows_bf16 = jax.ShapeDtypeStruct((n, FOX_WIDTH), BF16)
    cols_bf16 = jax.ShapeDtypeStruct((FOX_WIDTH, n), BF16)
    if for_flash:
        out_specs = [col, row(FOX_WIDTH), row(FOX_WIDTH), row(FOX_WIDTH), col]
        out_shape = [cols_bf16, rows_f32, rows_f32, rows_bf16, cols_bf16]
    else:
        out_specs = [row(FOX_WIDTH)] * 3
        out_shape = [rows_bf16, rows_f32, rows_f32]
    return pl.pallas_call(
        functools.partial(_qkv_kernel, for_flash=for_flash),
        grid=(n // tm,),
        in_specs=[row(d), pl.BlockSpec((1, d), lambda t: (0, 0)),
                  pl.BlockSpec((d, 3 * FOX_WIDTH), lambda t: (0, 0))],
        out_specs=out_specs,
        out_shape=out_shape,
        compiler_params=_cparams("parallel"),
        name="inproj_qkv",
    )(h, g, w_qkv)


def _split3(x):
    hi = x.astype(BF16)
    r = x - hi.astype(F32)
    mid = r.astype(BF16)
    lo = (r - mid.astype(F32)).astype(BF16)
    return hi, mid, lo


def _fmg_kernel(h_ref, g_ref, wf_ref, bf_ref, wmq_ref, wg_ref, mk_ref, mv_ref, *rest,
                tiles_per_seq, bias_cols):
    if bias_cols:
        tri_ref, sel_ref, logf_ref, m_ref, gates_ref, kf_ref, carry_sc = rest
    else:
        logf_ref, m_ref, gates_ref = rest
    tm = h_ref.shape[0]
    xn = _rms(h_ref[...], g_ref[...]).astype(BF16)
    zf = _dot(xn, wf_ref[...]) + bf_ref[...]
    logf = jnp.minimum(zf, 0.0) - jnp.log1p(jnp.exp(-jnp.abs(zf)))
    logf_ref[...] = logf[:, :FOX_HEADS]
    if bias_cols:
        @pl.when(pl.program_id(0) % tiles_per_seq == 0)
        def _():
            carry_sc[...] = jnp.zeros_like(carry_sc)

        lane = lax.broadcasted_iota(jnp.int32, logf.shape, 1)
        x = jnp.where(lane < FOX_HEADS, logf, 0.0)
        r = _dot(tri_ref[...], jnp.concatenate(_split3(x), axis=1))
        f = r[:, :LANES] + r[:, LANES:2 * LANES] + r[:, 2 * LANES:] + carry_sc[...]
        carry_sc[...] = f[tm - 1:tm, :]
        parts = jnp.concatenate(_split3(f * np.float32(-LOG2E)), axis=1)
        kf_ref[...] = _dot(parts, sel_ref[...]).astype(BF16)
    mq = _dot(xn, wmq_ref[...]).astype(BF16)
    for hh in range(MEM_HEADS):
        cols = slice(hh * MEM_HEAD_DIM, (hh + 1) * MEM_HEAD_DIM)
        s = _dot_nt(mq[:, cols], mk_ref[:, cols].astype(BF16)) * np.float32(MEM_HEAD_DIM ** -0.5)
        p = jnp.exp(s - jnp.max(s, axis=-1, keepdims=True))
        p = (p / jnp.sum(p, axis=-1, keepdims=True)).astype(BF16)
        m_ref[:, cols] = _dot(p, mv_ref[:, cols].astype(BF16)).astype(m_ref.dtype)
    gates_ref[...] = _sigmoid(_dot(xn, wg_ref[...])).astype(gates_ref.dtype)


def _fmg_call(h, g, w_f, b_f, w_mq, w_g, mk, mv, batch, tm, gates_dtype, bias_cols):
    n, d = h.shape
    seq = n // batch
    tpb = seq // tm
    n_mem = mk.shape[0] // batch
    const = lambda *shape: pl.BlockSpec(shape, lambda t: (0,) * len(shape))
    row = lambda width: pl.BlockSpec((tm, width), lambda t: (t, 0))
    args = [h, g, w_f, b_f, w_mq, w_g, mk, mv]
    in_specs = [
        row(d), const(1, d), const(d, LANES), const(1, LANES), const(d, MEM_WIDTH),
        const(d, N_BRANCH * d),
        pl.BlockSpec((n_mem, MEM_WIDTH), lambda t: (t // tpb, 0)),
        pl.BlockSpec((n_mem, MEM_WIDTH), lambda t: (t // tpb, 0)),
    ]
    out_specs = [row(FOX_HEADS), row(MEM_WIDTH), row(N_BRANCH * d)]
    out_shape = [
        jax.ShapeDtypeStruct((n, FOX_HEADS), F32),
        jax.ShapeDtypeStruct((n, MEM_WIDTH), BF16),
        jax.ShapeDtypeStruct((n, N_BRANCH * d), gates_dtype),
    ]
    scratch = []
    if bias_cols:
        tri = jnp.tril(jnp.ones((tm, tm), BF16))
        head = jnp.arange(FOX_HEADS)
        sel = jnp.zeros((3 * LANES, FOX_WIDTH), BF16)
        for c in range(3):
            sel = sel.at[c * LANES + head, head * FOX_HEAD_DIM + c].set(1)
        args += [tri, sel]
        in_specs += [const(tm, tm), const(3 * LANES, FOX_WIDTH)]
        out_specs.append(row(FOX_WIDTH))
        out_shape.append(jax.ShapeDtypeStruct((n, FOX_WIDTH), BF16))
        scratch.append(pltpu.VMEM((1, LANES), F32))
    return pl.pallas_call(
        functools.partial(_fmg_kernel, tiles_per_seq=tpb, bias_cols=bias_cols),
        grid=(n // tm,),
        in_specs=in_specs,
        out_specs=out_specs,
        out_shape=out_shape,
        scratch_shapes=scratch,
        compiler_params=_cparams("arbitrary" if bias_cols else "parallel"),
        name="inproj_logf_mem_gates",
    )(*args)


def _memkv_kernel(x_ref, g_ref, w_ref, mk_ref, mv_ref):
    xn = _rms(x_ref[...], g_ref[...]).astype(BF16)
    mk_ref[...] = _dot(xn, w_ref[:, :MEM_WIDTH])
    mv_ref[...] = _dot(xn, w_ref[:, MEM_WIDTH:])


def _memkv_call(mem, g, w_kv, tm):
    n, d = mem.shape
    return pl.pallas_call(
        _memkv_kernel,
        grid=(n // tm,),
        in_specs=[pl.BlockSpec((tm, d), lambda t: (t, 0)),
                  pl.BlockSpec((1, d), lambda t: (0, 0)),
                  pl.BlockSpec((d, 2 * MEM_WIDTH), lambda t: (0, 0))],
        out_specs=[pl.BlockSpec((tm, MEM_WIDTH), lambda t: (t, 0))] * 2,
        out_shape=[jax.ShapeDtypeStruct((n, MEM_WIDTH), F32)] * 2,
        compiler_params=_cparams("parallel"),
        name="memory_kv",
    )(mem, g, w_kv)


def _neg_cumsum_kernel(x_ref, o_ref):
    x = x_ref[...]
    width = x.shape[-1]
    lane = lax.broadcasted_iota(jnp.int32, x.shape, 1)
    d = 1
    while d < width:
        x = x + jnp.where(lane >= d, pltpu.roll(x, d, 1), 0.0)
        d *= 2
    o_ref[...] = -x


def _neg_cumsum_call(x):
    return pl.pallas_call(
        _neg_cumsum_kernel,
        out_shape=jax.ShapeDtypeStruct(x.shape, F32),
        compiler_params=pltpu.CompilerParams(vmem_limit_bytes=VMEM_LIMIT),
        name="logf_cumsum",
    )(x)


def _fox_kernel(qi_ref, kj_ref, flag_ref, qt_ref, ones_ref, k_ref, kf_ref, vt_ref, o_ref,
                m_sc, acc_sc, *, tile, group, heads):
    step = pl.program_id(2)
    flags = flag_ref[step]
    acc_rows = FOX_HEAD_DIM + BF16_ROWS

    @pl.when(flags & 1 != 0)
    def _():
        m_sc[...] = jnp.full_like(m_sc, -jnp.inf)
        acc_sc[...] = jnp.zeros_like(acc_sc)

    def update(diagonal):
        n_groups = tile // group
        scores = []
        for hd in range(heads):
            hs = slice(hd * FOX_HEAD_DIM, (hd + 1) * FOX_HEAD_DIM)
            ka = jnp.concatenate([k_ref[:, hs], kf_ref[:, hs]], axis=1)
            for gi in range(n_groups):
                cols = slice(gi * group, (gi + 1) * group)
                rows = (gi + 1) * group if diagonal else tile
                qa = jnp.concatenate([qt_ref[hs, cols], ones_ref[:, :group]], axis=0)
                s = _dot(ka[:rows], qa)
                if diagonal:
                    key = lax.broadcasted_iota(jnp.int32, s.shape, 0)
                    qry = lax.broadcasted_iota(jnp.int32, s.shape, 1) + gi * group
                    s = jnp.where(key <= qry, s, -jnp.inf)
                scores.append(s)
        for hd in range(heads):
            hs = slice(hd * FOX_HEAD_DIM, (hd + 1) * FOX_HEAD_DIM)
            ar = slice(hd * acc_rows, (hd + 1) * acc_rows)
            va = jnp.concatenate([vt_ref[hs, :], ones_ref[:BF16_ROWS]], axis=0)
            for gi in range(n_groups):
                s = scores[hd * n_groups + gi]
                cols = slice(gi * group, (gi + 1) * group)
                rows = s.shape[0]
                m_prev = m_sc[hd:hd + 1, cols]
                m_new = jnp.maximum(m_prev, jnp.max(s, axis=0, keepdims=True))
                p = jnp.exp2(s - m_new).astype(BF16)
                acc_sc[ar, cols] = (jnp.exp2(m_prev - m_new) * acc_sc[ar, cols]
                                    + _dot(va[:, :rows], p))
                m_sc[hd:hd + 1, cols] = m_new

    @pl.when(flags & 2 == 0)
    def _():
        update(False)

    @pl.when(flags & 2 != 0)
    def _():
        update(True)
        for hd in range(heads):
            acc = acc_sc[hd * acc_rows:(hd + 1) * acc_rows]
            out = acc[:FOX_HEAD_DIM] * (1.0 / acc[FOX_HEAD_DIM:FOX_HEAD_DIM + 1])
            o_ref[:, hd * FOX_HEAD_DIM:(hd + 1) * FOX_HEAD_DIM] = out.T.astype(o_ref.dtype)


def _fox_tables(n_tiles):
    qi, kj, flags = [], [], []
    for i in range(n_tiles):
        for j in range(i + 1):
            qi.append(i)
            kj.append(j)
            flags.append((1 if j == 0 else 0) | (2 if j == i else 0))
    return (np.asarray(qi, np.int32), np.asarray(kj, np.int32), np.asarray(flags, np.int32))


def _fox_call(q_t, k, kf, v_t, batch, tile, group, heads_per_step):
    width, n = q_t.shape
    seq = n // batch
    hw = heads_per_step * FOX_HEAD_DIM
    nt = seq // tile
    qi, kj, flags = _fox_tables(nt)
    ones = jnp.zeros((FOX_HEAD_DIM, tile), BF16).at[:3].set(1)
    q_map = lambda b, h, s, qi, kj, fl: (h, b * nt + qi[s])
    kt_map = lambda b, h, s, qi, kj, fl: (h, b * nt + kj[s])
    k_map = lambda b, h, s, qi, kj, fl: (b * nt + kj[s], h)
    grid_spec = pltpu.PrefetchScalarGridSpec(
        num_scalar_prefetch=3,
        grid=(batch, width // hw, len(qi)),
        in_specs=[
            pl.BlockSpec((hw, tile), q_map),
            pl.BlockSpec((FOX_HEAD_DIM, tile), lambda b, h, s, qi, kj, fl: (0, 0)),
            pl.BlockSpec((tile, hw), k_map),
            pl.BlockSpec((tile, hw), k_map),
            pl.BlockSpec((hw, tile), kt_map),
        ],
        out_specs=pl.BlockSpec((tile, hw), lambda b, h, s, qi, kj, fl: (b * nt + qi[s], h)),
        scratch_shapes=[
            pltpu.VMEM((heads_per_step, tile), F32),
            pltpu.VMEM((heads_per_step * (FOX_HEAD_DIM + BF16_ROWS), tile), F32),
        ],
    )
    return pl.pallas_call(
        functools.partial(_fox_kernel, tile=tile, group=group, heads=heads_per_step),
        grid_spec=grid_spec,
        out_shape=jax.ShapeDtypeStruct((n, width), BF16),
        compiler_params=_cparams("parallel", "parallel", "arbitrary"),
        name="fox_attention",
    )(jnp.asarray(qi), jnp.asarray(kj), jnp.asarray(flags), q_t, ones, k, kf, v_t)


def _fox_sample_kernel(q_ref, kc_ref, vc_ref, kn_ref, vn_ref, nf_ref, o_ref):
    t = q_ref.shape[0]
    past = kc_ref.shape[0]
    q = q_ref[...]
    nf = nf_ref[0]
    pad = jnp.zeros((LANES - t, FOX_HEAD_DIM), BF16)
    kn = jnp.concatenate([kn_ref[...].astype(BF16), pad], axis=0)
    vn = jnp.concatenate([vn_ref[...].astype(BF16), pad], axis=0)
    scale = np.float32(FOX_HEAD_DIM ** -0.5)
    s_c = _dot_nt(q, kc_ref[...].astype(BF16)) * scale + nf[:, :past]
    s_n = _dot_nt(q, kn) * scale + nf[:, past:]
    row = lax.broadcasted_iota(jnp.int32, (t, LANES), 0)
    col = lax.broadcasted_iota(jnp.int32, (t, LANES), 1)
    s_n = jnp.where(col <= row, s_n, -jnp.inf)
    m = jnp.maximum(jnp.max(s_c, axis=-1, keepdims=True), jnp.max(s_n, axis=-1, keepdims=True))
    p_c = jnp.exp(s_c - m)
    p_n = jnp.exp(s_n - m)
    l = jnp.sum(p_c, axis=-1, keepdims=True) + jnp.sum(p_n, axis=-1, keepdims=True)
    p_c = (p_c / l).astype(BF16)
    p_n = (p_n / l).astype(BF16)
    o_ref[...] = (_dot(p_c, vc_ref[...].astype(BF16)) + _dot(p_n, vn)).astype(o_ref.dtype)


def _fox_sample_call(q, k_cache, v_cache, layer, k_new, v_new, neg_f, batch, past):
    n, width = q.shape
    t = n // batch
    heads = width // FOX_HEAD_DIM
    nf = neg_f.reshape(batch * heads, 1, past + LANES)
    new = pl.BlockSpec((t, FOX_HEAD_DIM), lambda b, h: (b, h))
    old = pl.BlockSpec((past, FOX_HEAD_DIM), lambda b, h: (layer * batch + b, h))
    return pl.pallas_call(
        _fox_sample_kernel,
        grid=(batch, heads),
        in_specs=[new, old, old, new, new,
                  pl.BlockSpec((1, 1, past + LANES), lambda b, h: (b * heads + h, 0, 0))],
        out_specs=new,
        out_shape=jax.ShapeDtypeStruct((n, width), BF16),
        compiler_params=_cparams("parallel", "parallel"),
        name="fox_attention_sample",
    )(q, k_cache, v_cache, k_new, v_new, nf)


def _merge_kernel(h_ref, a_ref, b_ref, m_ref, gates_ref, wpa_ref, wpb_ref, wpm_ref, wo_ref, o_ref):
    d = h_ref.shape[1]
    y = gates_ref[:, :d].astype(F32) * _dot(a_ref[...], wpa_ref[...])
    y = y + gates_ref[:, d:2 * d].astype(F32) * _dot(b_ref[...], wpb_ref[...])
    y = y + gates_ref[:, 2 * d:].astype(F32) * _dot(m_ref[...], wpm_ref[...])
    o_ref[...] = h_ref[...] + _dot(y.astype(BF16), wo_ref[...])


def _merge_call(h, a, b, m, gates, w_pa, w_pb, w_pm, w_o, tm):
    n, d = h.shape
    row = lambda width: pl.BlockSpec((tm, width), lambda t: (t, 0))
    const = lambda w: pl.BlockSpec(w.shape, lambda t: (0, 0))
    return pl.pallas_call(
        _merge_kernel,
        grid=(n // tm,),
        in_specs=[row(d), row(GMLP_WIDTH), row(FOX_WIDTH), row(MEM_WIDTH), row(N_BRANCH * d),
                  const(w_pa), const(w_pb), const(w_pm), const(w_o)],
        out_specs=row(d),
        out_shape=jax.ShapeDtypeStruct((n, d), F32),
        compiler_params=_cparams("parallel"),
        name="merge_branches",
    )(h, a, b, m, gates, w_pa, w_pb, w_pm, w_o)


def _top2(logits, n_exp):
    lane = lax.broadcasted_iota(jnp.int32, logits.shape, 1)
    neg = jnp.float32(-jnp.inf)
    logits = jnp.where(lane < n_exp, logits, neg)
    v1 = jnp.max(logits, axis=-1, keepdims=True)
    i1 = jnp.min(jnp.where(logits == v1, lane, LANES), axis=-1, keepdims=True)
    rest = jnp.where(lane == i1, neg, logits)
    v2 = jnp.max(rest, axis=-1, keepdims=True)
    i2 = jnp.min(jnp.where(rest == v2, lane, LANES), axis=-1, keepdims=True)
    e2 = jnp.exp(v2 - v1)
    return lane, i1, i2, 1.0 / (1.0 + e2), e2 / (1.0 + e2)


def _top2_combine(logits, n_exp):
    lane, i1, i2, p1, p2 = _top2(logits, n_exp)
    return jnp.where(lane == i1, p1, 0.0) + jnp.where(lane == i2, p2, 0.0)


def _mixer_kernel(h_ref, g_ref, *refs, routed, final_norm):
    refs = list(refs)
    wr_ref = refs.pop(0) if routed else None
    gfin_ref = refs.pop(0) if final_norm else None
    w1_ref, w3_ref, w2_ref, o_ref, xn_sc, acc_sc = refs[:6]
    comb_sc = refs[6] if routed else None
    e = pl.program_id(1)
    c = pl.program_id(2)
    first = jnp.logical_and(e == 0, c == 0)
    last = jnp.logical_and(e == pl.num_programs(1) - 1, c == pl.num_programs(2) - 1)

    @pl.when(first)
    def _():
        xn = _rms(h_ref[...], g_ref[...])
        xn_sc[...] = xn.astype(BF16)
        acc_sc[...] = jnp.zeros_like(acc_sc)
        if routed:
            logits = _dot(xn.astype(BF16), wr_ref[...])
            comb_sc[...] = _top2_combine(logits, pl.num_programs(1))

    xb = xn_sc[...]
    g1 = _dot(xb, w1_ref[0])
    g3 = _dot(xb, w3_ref[0])
    mid = (g1 * _sigmoid(g1) * g3).astype(BF16)
    y = _dot(mid, w2_ref[0])
    if routed:
        lane = lax.broadcasted_iota(jnp.int32, comb_sc.shape, 1)
        y = y * jnp.sum(jnp.where(lane == e, comb_sc[...], 0.0), axis=-1, keepdims=True)
    acc_sc[...] += y

    @pl.when(last)
    def _():
        out = h_ref[...] + acc_sc[...]
        if final_norm:
            out = _rms(out, gfin_ref[...])
        o_ref[...] = out


def _mixer_call(h, g, w1, w3, w2, tm, tf, w_router=None, g_final=None):
    n, d = h.shape
    n_exp, _, ff = w1.shape
    routed = w_router is not None
    final_norm = g_final is not None
    row = pl.BlockSpec((tm, d), lambda t, e, c: (t, 0))
    vec = pl.BlockSpec((1, d), lambda t, e, c: (0, 0))
    args, in_specs = [h, g], [row, vec]
    if routed:
        args.append(w_router)
        in_specs.append(pl.BlockSpec(w_router.shape, lambda t, e, c: (0, 0)))
    if final_norm:
        args.append(g_final)
        in_specs.append(vec)
    args += [w1, w3, w2]
    in_specs += [
        pl.BlockSpec((1, d, tf), lambda t, e, c: (e, 0, c)),
        pl.BlockSpec((1, d, tf), lambda t, e, c: (e, 0, c)),
        pl.BlockSpec((1, tf, d), lambda t, e, c: (e, c, 0)),
    ]
    scratch = [pltpu.VMEM((tm, d), BF16), pltpu.VMEM((tm, d), F32)]
    if routed:
        scratch.append(pltpu.VMEM((tm, LANES), F32))
    return pl.pallas_call(
        functools.partial(_mixer_kernel, routed=routed, final_norm=final_norm),
        grid=(n // tm, n_exp, ff // tf),
        in_specs=in_specs,
        out_specs=row,
        out_shape=jax.ShapeDtypeStruct((n, d), F32),
        scratch_shapes=scratch,
        compiler_params=_cparams("parallel", "arbitrary", "arbitrary"),
        name="channel_mixer",
    )(*args)


def _router_kernel(h_ref, g_ref, wr_ref, idx_ref, p1_ref, p2_ref, *, n_exp):
    xn = _rms(h_ref[...], g_ref[...]).astype(BF16)
    lane, i1, i2, p1, p2 = _top2(_dot(xn, wr_ref[...]), n_exp)
    idx_ref[...] = jnp.where(lane == 0, i1, jnp.where(lane == 1, i2, 0))
    p1_ref[...] = jnp.broadcast_to(p1, p1_ref.shape)
    p2_ref[...] = jnp.broadcast_to(p2, p2_ref.shape)


def _router_call(h, g, w_router, n_exp, tm):
    n, d = h.shape
    wide = pl.BlockSpec((tm, LANES), lambda t: (t, 0))
    return pl.pallas_call(
        functools.partial(_router_kernel, n_exp=n_exp),
        grid=(n // tm,),
        in_specs=[pl.BlockSpec((tm, d), lambda t: (t, 0)), pl.BlockSpec((1, d), lambda t: (0, 0)),
                  pl.BlockSpec((d, LANES), lambda t: (0, 0))],
        out_specs=[wide, wide, wide],
        out_shape=[jax.ShapeDtypeStruct((n, LANES), jnp.int32),
                   jax.ShapeDtypeStruct((n, LANES), F32),
                   jax.ShapeDtypeStruct((n, LANES), F32)],
        compiler_params=_cparams("parallel"),
        name="moe_router",
    )(h, g, w_router)


def _row_copy(src_ref, src_row, dst_ref, dst_row, sem):
    src = src_ref.at[pl.ds(pl.multiple_of(src_row * SUBLANES, SUBLANES), SUBLANES)]
    dst = dst_ref.at[pl.ds(pl.multiple_of(dst_row * SUBLANES, SUBLANES), SUBLANES)]
    return pltpu.make_async_copy(src, dst, sem)


def _slab(s, rows):
    return (pl.ds(s, rows, stride=SUBLANES), slice(None))


def _dispatch_kernel(pos1_ref, pos2_ref, h_ref, g_ref, zero_ref, xs_ref, x_sc, sem):
    del zero_ref
    tm, d = h_ref.shape
    base = pl.program_id(0) * tm
    x = _rms(h_ref[...], g_ref[...])
    for s in range(d // LANES):
        x_sc[_slab(s, tm)] = x[:, s * LANES:(s + 1) * LANES]

    def start(i, carry):
        _row_copy(x_sc, i, xs_ref, pos1_ref[base + i], sem).start(priority=0)
        _row_copy(x_sc, i, xs_ref, pos2_ref[base + i], sem).start(priority=1)
        return carry

    def wait(i, carry):
        _row_copy(x_sc, i, xs_ref, pos1_ref[base + i], sem).wait()
        _row_copy(x_sc, i, xs_ref, pos2_ref[base + i], sem).wait()
        return carry

    lax.fori_loop(0, tm, start, 0)
    lax.fori_loop(0, tm, wait, 0)


def _dispatch_call(h, g, pos1, pos2, n_rows, tm):
    n, d = h.shape
    grid_spec = pltpu.PrefetchScalarGridSpec(
        num_scalar_prefetch=2,
        grid=(n // tm,),
        in_specs=[pl.BlockSpec((tm, d), lambda t, p1, p2: (t, 0)),
                  pl.BlockSpec((1, d), lambda t, p1, p2: (0, 0)),
                  pl.BlockSpec(memory_space=pl.ANY)],
        out_specs=pl.BlockSpec(memory_space=pl.ANY),
        scratch_shapes=[pltpu.VMEM((tm * SUBLANES, LANES), F32), pltpu.SemaphoreType.DMA(())],
    )
    assert d == SUBLANES * LANES
    return pl.pallas_call(
        _dispatch_kernel,
        grid_spec=grid_spec,
        out_shape=jax.ShapeDtypeStruct((n_rows * SUBLANES, LANES), F32),
        input_output_aliases={4: 0},
        compiler_params=_cparams("arbitrary"),
        name="moe_dispatch",
    )(pos1, pos2, h, g, jnp.zeros((n_rows * SUBLANES, LANES), F32))


def _experts_kernel(te_ref, nact_ref, x_ref, w1_ref, w3_ref, w2_ref, y_ref, xb_sc, acc_sc):
    t = pl.program_id(0)
    c = pl.program_id(1)

    tm, d = acc_sc.shape

    @pl.when(c == 0)
    def _():
        for s in range(d // LANES):
            xb_sc[:, s * LANES:(s + 1) * LANES] = x_ref[_slab(s, tm)].astype(BF16)
        acc_sc[...] = jnp.zeros_like(acc_sc)

    @pl.when(t < nact_ref[0])
    def _():
        xb = xb_sc[...]
        g1 = _dot(xb, w1_ref[0])
        g3 = _dot(xb, w3_ref[0])
        mid = (g1 * _sigmoid(g1) * g3).astype(BF16)
        acc_sc[...] += _dot(mid, w2_ref[0])

    @pl.when(c == pl.num_programs(1) - 1)
    def _():
        for s in range(d // LANES):
            y_ref[_slab(s, tm)] = acc_sc[:, s * LANES:(s + 1) * LANES]


def _experts_call(xs, tile_expert, n_active, w1, w3, w2, tm, tf):
    n_rows = xs.shape[0] // SUBLANES
    d = w1.shape[1]
    ff = w1.shape[2]
    nff = ff // tf

    def chunk(t, c, te, nact):
        return jnp.where(t < nact[0], c, nff - 1)

    grid_spec = pltpu.PrefetchScalarGridSpec(
        num_scalar_prefetch=2,
        grid=(n_rows // tm, nff),
        in_specs=[
            pl.BlockSpec((tm * SUBLANES, LANES), lambda t, c, te, nact: (t, 0)),
            pl.BlockSpec((1, d, tf), lambda t, c, te, nact: (te[t], 0, chunk(t, c, te, nact))),
            pl.BlockSpec((1, d, tf), lambda t, c, te, nact: (te[t], 0, chunk(t, c, te, nact))),
            pl.BlockSpec((1, tf, d), lambda t, c, te, nact: (te[t], chunk(t, c, te, nact), 0)),
        ],
        out_specs=pl.BlockSpec((tm * SUBLANES, LANES), lambda t, c, te, nact: (t, 0)),
        scratch_shapes=[pltpu.VMEM((tm, d), BF16), pltpu.VMEM((tm, d), F32)],
    )
    return pl.pallas_call(
        _experts_kernel,
        grid_spec=grid_spec,
        out_shape=jax.ShapeDtypeStruct((n_rows * SUBLANES, LANES), F32),
        compiler_params=_cparams("parallel", "arbitrary"),
        name="moe_experts",
    )(tile_expert, n_active, xs, w1, w3, w2)


def _combine_kernel(pos1_ref, pos2_ref, h_ref, p1_ref, p2_ref, *refs, final_norm):
    refs = list(refs)
    gfin_ref = refs.pop(0) if final_norm else None
    ys_ref, o_ref, y1_sc, y2_sc, sem = refs
    tm, d = h_ref.shape
    base = pl.program_id(0) * tm

    def start(i, carry):
        _row_copy(ys_ref, pos1_ref[base + i], y1_sc, i, sem).start(priority=0)
        _row_copy(ys_ref, pos2_ref[base + i], y2_sc, i, sem).start(priority=1)
        return carry

    def wait(i, carry):
        _row_copy(ys_ref, pos1_ref[base + i], y1_sc, i, sem).wait()
        _row_copy(ys_ref, pos2_ref[base + i], y2_sc, i, sem).wait()
        return carry

    lax.fori_loop(0, tm, start, 0)
    lax.fori_loop(0, tm, wait, 0)
    w1 = p1_ref[...]
    w2 = p2_ref[...]
    out = jnp.concatenate(
        [h_ref[:, s * LANES:(s + 1) * LANES]
         + (w1 * y1_sc[_slab(s, tm)] + w2 * y2_sc[_slab(s, tm)]) for s in range(d // LANES)], axis=1)
    if final_norm:
        out = _rms(out, gfin_ref[...])
    o_ref[...] = out


def _combine_call(h, p1, p2, ys, pos1, pos2, tm, g_final=None):
    n, d = h.shape
    final_norm = g_final is not None
    row = pl.BlockSpec((tm, d), lambda t, a, b: (t, 0))
    wide = pl.BlockSpec((tm, LANES), lambda t, a, b: (t, 0))
    args, in_specs = [h, p1, p2], [row, wide, wide]
    if final_norm:
        args.append(g_final)
        in_specs.append(pl.BlockSpec((1, d), lambda t, a, b: (0, 0)))
    args.append(ys)
    in_specs.append(pl.BlockSpec(memory_space=pl.ANY))
    grid_spec = pltpu.PrefetchScalarGridSpec(
        num_scalar_prefetch=2,
        grid=(n // tm,),
        in_specs=in_specs,
        out_specs=row,
        scratch_shapes=[pltpu.VMEM((tm * SUBLANES, LANES), F32),
                        pltpu.VMEM((tm * SUBLANES, LANES), F32),
                        pltpu.SemaphoreType.DMA(())],
    )
    return pl.pallas_call(
        functools.partial(_combine_kernel, final_norm=final_norm),
        grid_spec=grid_spec,
        out_shape=jax.ShapeDtypeStruct((n, d), F32),
        compiler_params=_cparams("arbitrary"),
        name="moe_combine",
    )(pos1, pos2, *args)


def _routing_tables(idx, n_exp, tm):
    n = idx.shape[0]
    e = idx[:, :TOP_K].reshape(-1)
    onehot = (e[:, None] == jnp.arange(n_exp, dtype=jnp.int32)[None, :]).astype(jnp.int32)
    csum = jnp.cumsum(onehot, axis=0)
    rank = jnp.take_along_axis(csum, e[:, None], axis=1)[:, 0] - 1
    counts = csum[-1]
    padded = (counts + tm - 1) // tm * tm
    seg_end = jnp.cumsum(padded)
    pos = ((seg_end - padded)[e] + rank).reshape(n, TOP_K)
    n_rows = TOP_K * n + n_exp * tm
    tile_first = jnp.arange(n_rows // tm, dtype=jnp.int32) * tm
    tile_expert = jnp.minimum(jnp.sum(tile_first[:, None] >= seg_end[None, :], axis=1), n_exp - 1)
    n_active = (seg_end[-1] // tm).reshape(1)
    return pos[:, 0], pos[:, 1], tile_expert.astype(jnp.int32), n_active.astype(jnp.int32), n_rows


def _routed_moe(h, g, w_router, w1, w3, w2, tm_tok, tm_rows, tf, g_final):
    n_exp = w1.shape[0]
    idx, p1, p2 = _router_call(h, g, w_router, n_exp, tm_tok)
    pos1, pos2, tile_expert, n_active, n_rows = _routing_tables(idx, n_exp, tm_rows)
    xs = _dispatch_call(h, g, pos1, pos2, n_rows, tm_tok)
    ys = _experts_call(xs, tile_expert, n_active, w1, w3, w2, tm_rows, tf)
    return _combine_call(h, p1, p2, ys, pos1, pos2, tm_tok, g_final)


def _largest_tile(n, cap, quantum):
    best = quantum
    t = quantum
    while t <= min(n, cap):
        if n % t == 0:
            best = t
        t += quantum
    return best


def _spatial_operands(w_s, b_s, blk):
    reps = GMLP_BLOCK // blk
    w = w_s[:, :blk, :blk] * jnp.tril(jnp.ones((blk, blk), w_s.dtype))
    eye = jnp.eye(reps, dtype=w_s.dtype)
    w = jnp.einsum("ab,gts->gatbs", eye, w).reshape(GMLP_GROUPS, GMLP_BLOCK, GMLP_BLOCK)
    b = jnp.tile(b_s[:, :blk], (1, reps))
    ws_pairs = jnp.concatenate([w[0::2], w[1::2]], axis=2).astype(BF16)
    bs_pairs = jnp.concatenate(
        [jnp.broadcast_to(b[0::2, :, None], (GMLP_GROUPS // 2, GMLP_BLOCK, GMLP_GROUP_DIM)),
         jnp.broadcast_to(b[1::2, :, None], (GMLP_GROUPS // 2, GMLP_BLOCK, GMLP_GROUP_DIM))],
        axis=2).astype(F32)
    return ws_pairs, bs_pairs


def kernel(x_prompt, x_sample, cache_fox_k, cache_fox_v, cache_fox_logf, cache_mem_k, cache_mem_v, mem_prompt, g_mix, w_in, b_f, ln_v_g, ln_v_b, w_s, b_s, g_mem, w_mem_kv, w_pa, w_pb, w_pm, w_o, g_ffn, w1, w3, w2, w_router, e_w1, e_w3, e_w2, g_final):
    depth = w_in.shape[0]
    bp, sp, d = x_prompt.shape
    bs, ts, _ = x_sample.shape
    past = cache_fox_k.shape[2]
    n_mem = mem_prompt.shape[1]
    n_exp = e_w1.shape[1]
    assert sp % GMLP_BLOCK == 0 and GMLP_BLOCK % ts == 0 and (bs * ts) % GMLP_BLOCK == 0

    off_q = 2 * GMLP_WIDTH
    off_f = off_q + 3 * FOX_WIDTH
    off_mq = off_f + FOX_HEADS
    off_g = off_mq + MEM_WIDTH

    hp = x_prompt.reshape(bp * sp, d)
    hs = x_sample.reshape(bs * ts, d)
    mem = mem_prompt.reshape(bp * n_mem, d)
    cache_k_rows = cache_fox_k.reshape(depth * bs * past, FOX_WIDTH)
    cache_v_rows = cache_fox_v.reshape(depth * bs * past, FOX_WIDTH)

    tm_p = _largest_tile(sp, 512, GMLP_BLOCK)
    tm_s = GMLP_BLOCK
    fox_group = 2 * LANES
    fox_heads = 4
    fox_tile = _largest_tile(sp, 1024, fox_group)
    tmix_p = _largest_tile(bp * sp, 1024, LANES)
    tmix_s = bs * ts

    outs = {k: [] for k in ("fk_p", "fv_p", "fl_p", "mk_p", "mv_p", "fk_s", "fv_s", "fl_s", "gv_s")}
    for i in range(depth):
        g = g_mix[i].reshape(1, d)
        wi = w_in[i]
        w_uv = wi[:, :off_q].astype(BF16)
        w_qkv = wi[:, off_q:off_f].astype(BF16)
        w_f = jnp.zeros((d, LANES), BF16).at[:, :FOX_HEADS].set(wi[:, off_f:off_mq].astype(BF16))
        w_mq = wi[:, off_mq:off_g].astype(BF16)
        w_g = wi[:, off_g:].astype(BF16)
        bfi = jnp.zeros((1, LANES), F32).at[0, :FOX_HEADS].set(b_f[i])
        lng = ln_v_g[i].reshape(1, GMLP_WIDTH)
        lnb = ln_v_b[i].reshape(1, GMLP_WIDTH)
        wpa, wpb, wpm, wo = (w[i].astype(BF16) for w in (w_pa, w_pb, w_pm, w_o))
        gf = g_ffn[i].reshape(1, d)
        j = i // 2
        if i % 2 == 0:
            mw1, mw3, mw2 = (w[j][None].astype(BF16) for w in (w1, w3, w2))
            wr = None
        else:
            mw1, mw3, mw2 = (w[j].astype(BF16) for w in (e_w1, e_w3, e_w2))
            wr = jnp.zeros((d, LANES), BF16).at[:, :n_exp].set(w_router[j].astype(BF16))
        ff = mw1.shape[2]
        tf = _largest_tile(ff, 1408, LANES)
        gfin = g_final.reshape(1, d) if i == depth - 1 else None

        ws_pairs, bs_pairs = _spatial_operands(w_s[i], b_s[i], GMLP_BLOCK)
        mk, mv = _memkv_call(mem, g_mem[i].reshape(1, d), w_mem_kv[i].astype(BF16), n_mem)
        a, _ = _uv_call(hp, g, w_uv, lng, lnb, ws_pairs, bs_pairs, tm_p)
        q_t, k, v, kb, v_t = _qkv_call(hp, g, w_qkv, tm_p, True)
        logf, m, gates, kf = _fmg_call(hp, g, w_f, bfi, w_mq, w_g, mk, mv, bp, tm_p, BF16, True)
        b = _fox_call(q_t, kb, kf, v_t, bp, fox_tile, fox_group, fox_heads)
        hp = _merge_call(hp, a, b, m, gates, wpa, wpb, wpm, wo, tm_p)
        if wr is None:
            hp = _mixer_call(hp, gf, mw1, mw3, mw2, tmix_p, tf, None, gfin)
        else:
            hp = _routed_moe(hp, gf, wr, mw1, mw3, mw2, tm_p, tmix_p, tf, gfin)
        outs["fk_p"].append(k.reshape(bp, sp, FOX_HEADS, FOX_HEAD_DIM))
        outs["fv_p"].append(v.reshape(bp, sp, FOX_HEADS, FOX_HEAD_DIM))
        outs["fl_p"].append(logf.reshape(bp, sp, FOX_HEADS))
        outs["mk_p"].append(mk.reshape(bp, n_mem, MEM_HEADS, MEM_HEAD_DIM))
        outs["mv_p"].append(mv.reshape(bp, n_mem, MEM_HEADS, MEM_HEAD_DIM))

        ws_pairs, bs_pairs = _spatial_operands(w_s[i], b_s[i], ts)
        a, vn = _uv_call(hs, g, w_uv, lng, lnb, ws_pairs, bs_pairs, tm_s)
        q, k, v = _qkv_call(hs, g, w_qkv, bs * ts, False)
        logf, m, gates = _fmg_call(hs, g, w_f, bfi, w_mq, w_g,
                                   cache_mem_k[i].reshape(bs * n_mem, MEM_WIDTH),
                                   cache_mem_v[i].reshape(bs * n_mem, MEM_WIDTH), bs, ts, F32, False)
        logf = logf.reshape(bs, ts, FOX_HEADS)
        logf_all = jnp.concatenate(
            [cache_fox_logf[i].transpose(0, 2, 1), logf.transpose(0, 2, 1),
             jnp.zeros((bs, FOX_HEADS, LANES - ts), F32)], axis=2)
        neg_f = _neg_cumsum_call(logf_all.reshape(bs * FOX_HEADS, past + LANES))
        b = _fox_sample_call(q, cache_k_rows, cache_v_rows, i, k, v, neg_f, bs, past)
        hs = _merge_call(hs, a, b, m, gates, wpa, wpb, wpm, wo, tmix_s)
        hs = _mixer_call(hs, gf, mw1, mw3, mw2, tmix_s, tf, wr, gfin)
        outs["fk_s"].append(k.reshape(bs, ts, FOX_HEADS, FOX_HEAD_DIM))
        outs["fv_s"].append(v.reshape(bs, ts, FOX_HEADS, FOX_HEAD_DIM))
        outs["fl_s"].append(logf)
        outs["gv_s"].append(vn.reshape(bs, ts, GMLP_WIDTH))

    st = {k: jnp.stack(v) for k, v in outs.items()}
    return (hp.reshape(bp, sp, d), hs.reshape(bs, ts, d),
            st["fk_p"], st["fv_p"], st["fl_p"], st["mk_p"], st["mv_p"],
            st["fk_s"], st["fv_s"], st["fl_s"], st["gv_s"])
```

```python
import functools

import numpy as np
import jax
import jax.numpy as jnp
from jax import lax
from jax.experimental import pallas as pl
from jax.experimental.pallas import tpu as pltpu

F32 = jnp.float32
BF16 = jnp.bfloat16

EPS = 1e-6
GMLP_GROUPS = 8
GMLP_GROUP_DIM = 64
GMLP_WIDTH = GMLP_GROUPS * GMLP_GROUP_DIM
GMLP_BLOCK = 128
FOX_HEADS = 8
FOX_HEAD_DIM = 128
FOX_WIDTH = FOX_HEADS * FOX_HEAD_DIM
MEM_HEADS = 4
MEM_HEAD_DIM = 128
MEM_WIDTH = MEM_HEADS * MEM_HEAD_DIM
N_BRANCH = 3
TOP_K = 2
LANES = 128
SUBLANES = 8
BF16_ROWS = 16
LOG2E = 1.4426950408889634
VMEM_LIMIT = 56 * 1024 * 1024

_NT = (((1,), (1,)), ((), ()))


def _cparams(*sem):
    return pltpu.CompilerParams(dimension_semantics=sem, vmem_limit_bytes=VMEM_LIMIT)


def _rms(x, g):
    return x * lax.rsqrt(jnp.mean(x * x, axis=-1, keepdims=True) + EPS) * g


def _sigmoid(x):
    return 1.0 / (1.0 + jnp.exp(-x))


def _dot(a, b):
    return jnp.dot(a, b, preferred_element_type=F32)


def _dot_nt(a, b):
    return lax.dot_general(a, b, _NT, preferred_element_type=F32)


def _uv_kernel(h_ref, g_ref, w_ref, lng_ref, lnb_ref, ws_ref, bs_ref, a_ref, vn_ref):
    tm = h_ref.shape[0]
    xn = _rms(h_ref[...], g_ref[...]).astype(BF16)
    z = _dot(xn, w_ref[...])
    uv = 0.5 * z * (1.0 + lax.erf(z * np.float32(2.0 ** -0.5)))
    u = uv[:, :GMLP_WIDTH]
    v = uv[:, GMLP_WIDTH:]
    mu = jnp.mean(v, axis=-1, keepdims=True)
    vc = v - mu
    var = jnp.mean(vc * vc, axis=-1, keepdims=True)
    vn = vc * lax.rsqrt(var + EPS) * lng_ref[...] + lnb_ref[...]
    vn_ref[...] = vn
    lane = lax.broadcasted_iota(jnp.int32, (GMLP_BLOCK, LANES), 1)
    low = lane < GMLP_GROUP_DIM
    for r in range(tm // GMLP_BLOCK):
        rows = slice(r * GMLP_BLOCK, (r + 1) * GMLP_BLOCK)
        for p in range(GMLP_WIDTH // LANES):
            cols = slice(p * LANES, (p + 1) * LANES)
            vs = vn[rows, cols].astype(BF16)
            zero = jnp.zeros_like(vs)
            rhs = jnp.concatenate([jnp.where(low, vs, zero), jnp.where(low, zero, vs)], axis=0)
            mixed = _dot(ws_ref[p], rhs) + bs_ref[p]
            a_ref[rows, cols] = (u[rows, cols] * mixed).astype(a_ref.dtype)


def _uv_call(h, g, w_uv, ln_g, ln_b, ws_pairs, bs_pairs, tm):
    n, d = h.shape
    const = lambda *shape: pl.BlockSpec(shape, lambda t: (0,) * len(shape))
    return pl.pallas_call(
        _uv_kernel,
        grid=(n // tm,),
        in_specs=[
            pl.BlockSpec((tm, d), lambda t: (t, 0)),
            const(1, d),
            const(d, 2 * GMLP_WIDTH),
            const(1, GMLP_WIDTH),
            const(1, GMLP_WIDTH),
            const(*ws_pairs.shape),
            const(*bs_pairs.shape),
        ],
        out_specs=[
            pl.BlockSpec((tm, GMLP_WIDTH), lambda t: (t, 0)),
            pl.BlockSpec((tm, GMLP_WIDTH), lambda t: (t, 0)),
        ],
        out_shape=[
            jax.ShapeDtypeStruct((n, GMLP_WIDTH), BF16),
            jax.ShapeDtypeStruct((n, GMLP_WIDTH), F32),
        ],
        compiler_params=_cparams("parallel"),
        name="inproj_gmlp",
    )(h, g, w_uv, ln_g, ln_b, ws_pairs, bs_pairs)


def _qkv_kernel(h_ref, g_ref, w_ref, q_ref, k_ref, v_ref, *bf_refs, for_flash):
    xn = _rms(h_ref[...], g_ref[...]).astype(BF16)
    q = _dot(xn, w_ref[:, :FOX_WIDTH])
    k = _dot(xn, w_ref[:, FOX_WIDTH:2 * FOX_WIDTH])
    v = _dot(xn, w_ref[:, 2 * FOX_WIDTH:])
    k_ref[...] = k
    v_ref[...] = v
    if for_flash:
        kb_ref, vt_ref = bf_refs
        q_ref[...] = (q * np.float32(FOX_HEAD_DIM ** -0.5 * LOG2E)).T.astype(BF16)
        kb_ref[...] = k.astype(BF16)
        vt_ref[...] = v.T.astype(BF16)
    else:
        q_ref[...] = q.astype(BF16)


def _qkv_call(h, g, w_qkv, tm, for_flash):
    n, d = h.shape
    row = lambda width: pl.BlockSpec((tm, width), lambda t: (t, 0))
    col = pl.BlockSpec((FOX_WIDTH, tm), lambda t: (0, t))
    rows_f32 = jax.ShapeDtypeStruct((n, FOX_WIDTH), F32)
    rows_bf16 = jax.ShapeDtypeStruct((n, FOX_WIDTH), BF16)
    cols_bf16 = jax.ShapeDtypeStruct((FOX_WIDTH, n), BF16)
    if for_flash:
        out_specs = [col, row(FOX_WIDTH), row(FOX_WIDTH), row(FOX_WIDTH), col]
        out_shape = [cols_bf16, rows_f32, rows_f32, rows_bf16, cols_bf16]
    else:
        out_specs = [row(FOX_WIDTH)] * 3
        out_shape = [rows_bf16, rows_f32, rows_f32]
    return pl.pallas_call(
        functools.partial(_qkv_kernel, for_flash=for_flash),
        grid=(n // tm,),
        in_specs=[row(d), pl.BlockSpec((1, d), lambda t: (0, 0)),
                  pl.BlockSpec((d, 3 * FOX_WIDTH), lambda t: (0, 0))],
        out_specs=out_specs,
        out_shape=out_shape,
        compiler_params=_cparams("parallel"),
        name="inproj_qkv",
    )(h, g, w_qkv)


def _split3(x):
    hi = x.astype(BF16)
    r = x - hi.astype(F32)
    mid = r.astype(BF16)
    lo = (r - mid.astype(F32)).astype(BF16)
    return hi, mid, lo


def _fmg_kernel(h_ref, g_ref, wf_ref, bf_ref, wmq_ref, wg_ref, mk_ref, mv_ref, *rest,
                tiles_per_seq, bias_cols):
    if bias_cols:
        tri_ref, sel_ref, logf_ref, m_ref, gates_ref, kf_ref, carry_sc = rest
    else:
        logf_ref, m_ref, gates_ref = rest
    tm = h_ref.shape[0]
    xn = _rms(h_ref[...], g_ref[...]).astype(BF16)
    zf = _dot(xn, wf_ref[...]) + bf_ref[...]
    logf = jnp.minimum(zf, 0.0) - jnp.log1p(jnp.exp(-jnp.abs(zf)))
    logf_ref[...] = logf[:, :FOX_HEADS]
    if bias_cols:
        @pl.when(pl.program_id(0) % tiles_per_seq == 0)
        def _():
            carry_sc[...] = jnp.zeros_like(carry_sc)

        lane = lax.broadcasted_iota(jnp.int32, logf.shape, 1)
        x = jnp.where(lane < FOX_HEADS, logf, 0.0)
        r = _dot(tri_ref[...], jnp.concatenate(_split3(x), axis=1))
        f = r[:, :LANES] + r[:, LANES:2 * LANES] + r[:, 2 * LANES:] + carry_sc[...]
        carry_sc[...] = f[tm - 1:tm, :]
        parts = jnp.concatenate(_split3(f * np.float32(-LOG2E)), axis=1)
        kf_ref[...] = _dot(parts, sel_ref[...]).astype(BF16)
    mq = _dot(xn, wmq_ref[...]).astype(BF16)
    for hh in range(MEM_HEADS):
        cols = slice(hh * MEM_HEAD_DIM, (hh + 1) * MEM_HEAD_DIM)
        s = _dot_nt(mq[:, cols], mk_ref[:, cols].astype(BF16)) * np.float32(MEM_HEAD_DIM ** -0.5)
        p = jnp.exp(s - jnp.max(s, axis=-1, keepdims=True))
        p = (p / jnp.sum(p, axis=-1, keepdims=True)).astype(BF16)
        m_ref[:, cols] = _dot(p, mv_ref[:, cols].astype(BF16)).astype(m_ref.dtype)
    gates_ref[...] = _sigmoid(_dot(xn, wg_ref[...])).astype(gates_ref.dtype)


def _fmg_call(h, g, w_f, b_f, w_mq, w_g, mk, mv, batch, tm, gates_dtype, bias_cols):
    n, d = h.shape
    seq = n // batch
    tpb = seq // tm
    n_mem = mk.shape[0] // batch
    const = lambda *shape: pl.BlockSpec(shape, lambda t: (0,) * len(shape))
    row = lambda width: pl.BlockSpec((tm, width), lambda t: (t, 0))
    args = [h, g, w_f, b_f, w_mq, w_g, mk, mv]
    in_specs = [
        row(d), const(1, d), const(d, LANES), const(1, LANES), const(d, MEM_WIDTH),
        const(d, N_BRANCH * d),
        pl.BlockSpec((n_mem, MEM_WIDTH), lambda t: (t // tpb, 0)),
        pl.BlockSpec((n_mem, MEM_WIDTH), lambda t: (t // tpb, 0)),
    ]
    out_specs = [row(FOX_HEADS), row(MEM_WIDTH), row(N_BRANCH * d)]
    out_shape = [
        jax.ShapeDtypeStruct((n, FOX_HEADS), F32),
        jax.ShapeDtypeStruct((n, MEM_WIDTH), BF16),
        jax.ShapeDtypeStruct((n, N_BRANCH * d), gates_dtype),
    ]
    scratch = []
    if bias_cols:
        tri = jnp.tril(jnp.ones((tm, tm), BF16))
        head = jnp.arange(FOX_HEADS)
        sel = jnp.zeros((3 * LANES, FOX_WIDTH), BF16)
        for c in range(3):
            sel = sel.at[c * LANES + head, head * FOX_HEAD_DIM + c].set(1)
        args += [tri, sel]
        in_specs += [const(tm, tm), const(3 * LANES, FOX_WIDTH)]
        out_specs.append(row(FOX_WIDTH))
        out_shape.append(jax.ShapeDtypeStruct((n, FOX_WIDTH), BF16))
        scratch.append(pltpu.VMEM((1, LANES), F32))
    return pl.pallas_call(
        functools.partial(_fmg_kernel, tiles_per_seq=tpb, bias_cols=bias_cols),
        grid=(n // tm,),
        in_specs=in_specs,
        out_specs=out_specs,
        out_shape=out_shape,
        scratch_shapes=scratch,
        compiler_params=_cparams("arbitrary" if bias_cols else "parallel"),
        name="inproj_logf_mem_gates",
    )(*args)


def _memkv_kernel(x_ref, g_ref, w_ref, mk_ref, mv_ref):
    xn = _rms(x_ref[...], g_ref[...]).astype(BF16)
    mk_ref[...] = _dot(xn, w_ref[:, :MEM_WIDTH])
    mv_ref[...] = _dot(xn, w_ref[:, MEM_WIDTH:])


def _memkv_call(mem, g, w_kv, tm):
    n, d = mem.shape
    return pl.pallas_call(
        _memkv_kernel,
        grid=(n // tm,),
        in_specs=[pl.BlockSpec((tm, d), lambda t: (t, 0)),
                  pl.BlockSpec((1, d), lambda t: (0, 0)),
                  pl.BlockSpec((d, 2 * MEM_WIDTH), lambda t: (0, 0))],
        out_specs=[pl.BlockSpec((tm, MEM_WIDTH), lambda t: (t, 0))] * 2,
        out_shape=[jax.ShapeDtypeStruct((n, MEM_WIDTH), F32)] * 2,
        compiler_params=_cparams("parallel"),
        name="memory_kv",
    )(mem, g, w_kv)


def _neg_cumsum_kernel(x_ref, o_ref):
    x = x_ref[...]
    width = x.shape[-1]
    lane = lax.broadcasted_iota(jnp.int32, x.shape, 1)
    d = 1
    while d < width:
        x = x + jnp.where(lane >= d, pltpu.roll(x, d, 1), 0.0)
        d *= 2
    o_ref[...] = -x


def _neg_cumsum_call(x):
    return pl.pallas_call(
        _neg_cumsum_kernel,
        out_shape=jax.ShapeDtypeStruct(x.shape, F32),
        compiler_params=pltpu.CompilerParams(vmem_limit_bytes=VMEM_LIMIT),
        name="logf_cumsum",
    )(x)


def _fox_kernel(qi_ref, kj_ref, flag_ref, qt_ref, ones_ref, k_ref, kf_ref, vt_ref, o_ref,
                m_sc, acc_sc, *, tile, group, heads):
    step = pl.program_id(2)
    flags = flag_ref[step]
    acc_rows = FOX_HEAD_DIM + BF16_ROWS

    @pl.when(flags & 1 != 0)
    def _():
        m_sc[...] = jnp.full_like(m_sc, -jnp.inf)
        acc_sc[...] = jnp.zeros_like(acc_sc)

    def update(diagonal):
        n_groups = tile // group
        scores = []
        for hd in range(heads):
            hs = slice(hd * FOX_HEAD_DIM, (hd + 1) * FOX_HEAD_DIM)
            ka = jnp.concatenate([k_ref[:, hs], kf_ref[:, hs]], axis=1)
            for gi in range(n_groups):
                cols = slice(gi * group, (gi + 1) * group)
                rows = (gi + 1) * group if diagonal else tile
                qa = jnp.concatenate([qt_ref[hs, cols], ones_ref[:, :group]], axis=0)
                s = _dot(ka[:rows], qa)
                if diagonal:
                    key = lax.broadcasted_iota(jnp.int32, s.shape, 0)
                    qry = lax.broadcasted_iota(jnp.int32, s.shape, 1) + gi * group
                    s = jnp.where(key <= qry, s, -jnp.inf)
                scores.append(s)
        for hd in range(heads):
            hs = slice(hd * FOX_HEAD_DIM, (hd + 1) * FOX_HEAD_DIM)
            ar = slice(hd * acc_rows, (hd + 1) * acc_rows)
            va = jnp.concatenate([vt_ref[hs, :], ones_ref[:BF16_ROWS]], axis=0)
            for gi in range(n_groups):
                s = scores[hd * n_groups + gi]
                cols = slice(gi * group, (gi + 1) * group)
                rows = s.shape[0]
                m_prev = m_sc[hd:hd + 1, cols]
                m_new = jnp.maximum(m_prev, jnp.max(s, axis=0, keepdims=True))
                p = jnp.exp2(s - m_new).astype(BF16)
                acc_sc[ar, cols] = (jnp.exp2(m_prev - m_new) * acc_sc[ar, cols]
                                    + _dot(va[:, :rows], p))
                m_sc[hd:hd + 1, cols] = m_new

    @pl.when(flags & 2 == 0)
    def _():
        update(False)

    @pl.when(flags & 2 != 0)
    def _():
        update(True)
        for hd in range(heads):
            acc = acc_sc[hd * acc_rows:(hd + 1) * acc_rows]
            out = acc[:FOX_HEAD_DIM] * (1.0 / acc[FOX_HEAD_DIM:FOX_HEAD_DIM + 1])
            o_ref[:, hd * FOX_HEAD_DIM:(hd + 1) * FOX_HEAD_DIM] = out.T.astype(o_ref.dtype)


def _fox_tables(n_tiles):
    qi, kj, flags = [], [], []
    for i in range(n_tiles):
        for j in range(i + 1):
            qi.append(i)
            kj.append(j)
            flags.append((1 if j == 0 else 0) | (2 if j == i else 0))
    return (np.asarray(qi, np.int32), np.asarray(kj, np.int32), np.asarray(flags, np.int32))


def _fox_call(q_t, k, kf, v_t, batch, tile, group, heads_per_step):
    width, n = q_t.shape
    seq = n // batch
    hw = heads_per_step * FOX_HEAD_DIM
    nt = seq // tile
    qi, kj, flags = _fox_tables(nt)
    ones = jnp.zeros((FOX_HEAD_DIM, tile), BF16).at[:3].set(1)
    q_map = lambda b, h, s, qi, kj, fl: (h, b * nt + qi[s])
    kt_map = lambda b, h, s, qi, kj, fl: (h, b * nt + kj[s])
    k_map = lambda b, h, s, qi, kj, fl: (b * nt + kj[s], h)
    grid_spec = pltpu.PrefetchScalarGridSpec(
        num_scalar_prefetch=3,
        grid=(batch, width // hw, len(qi)),
        in_specs=[
            pl.BlockSpec((hw, tile), q_map),
            pl.BlockSpec((FOX_HEAD_DIM, tile), lambda b, h, s, qi, kj, fl: (0, 0)),
            pl.BlockSpec((tile, hw), k_map),
            pl.BlockSpec((tile, hw), k_map),
            pl.BlockSpec((hw, tile), kt_map),
        ],
        out_specs=pl.BlockSpec((tile, hw), lambda b, h, s, qi, kj, fl: (b * nt + qi[s], h)),
        scratch_shapes=[
            pltpu.VMEM((heads_per_step, tile), F32),
            pltpu.VMEM((heads_per_step * (FOX_HEAD_DIM + BF16_ROWS), tile), F32),
        ],
    )
    return pl.pallas_call(
        functools.partial(_fox_kernel, tile=tile, group=group, heads=heads_per_step),
        grid_spec=grid_spec,
        out_shape=jax.ShapeDtypeStruct((n, width), BF16),
        compiler_params=_cparams("parallel", "parallel", "arbitrary"),
        name="fox_attention",
    )(jnp.asarray(qi), jnp.asarray(kj), jnp.asarray(flags), q_t, ones, k, kf, v_t)


def _fox_sample_kernel(q_ref, kc_ref, vc_ref, kn_ref, vn_ref, nf_ref, o_ref, *, heads):
    t = q_ref.shape[0]
    past = kc_ref.shape[0] // heads
    scale = np.float32(FOX_HEAD_DIM ** -0.5)
    pad = jnp.zeros((LANES - t, FOX_HEAD_DIM), BF16)
    row = lax.broadcasted_iota(jnp.int32, (t, LANES), 0)
    col = lax.broadcasted_iota(jnp.int32, (t, LANES), 1)
    for hd in range(heads):
        hs = slice(hd * FOX_HEAD_DIM, (hd + 1) * FOX_HEAD_DIM)
        cached = (pl.ds(hd, past, stride=heads), slice(None))
        q = q_ref[:, hs]
        nf = nf_ref[hd]
        kn = jnp.concatenate([kn_ref[:, hs].astype(BF16), pad], axis=0)
        vn = jnp.concatenate([vn_ref[:, hs].astype(BF16), pad], axis=0)
        s_c = _dot_nt(q, kc_ref[cached].astype(BF16)) * scale + nf[:, :past]
        s_n = _dot_nt(q, kn) * scale + nf[:, past:]
        s_n = jnp.where(col <= row, s_n, -jnp.inf)
        m = jnp.maximum(jnp.max(s_c, axis=-1, keepdims=True), jnp.max(s_n, axis=-1, keepdims=True))
        p_c = jnp.exp(s_c - m)
        p_n = jnp.exp(s_n - m)
        l = jnp.sum(p_c, axis=-1, keepdims=True) + jnp.sum(p_n, axis=-1, keepdims=True)
        p_c = (p_c / l).astype(BF16)
        p_n = (p_n / l).astype(BF16)
        o_ref[:, hs] = (_dot(p_c, vc_ref[cached].astype(BF16)) + _dot(p_n, vn)).astype(o_ref.dtype)


def _fox_sample_call(q, k_cache, v_cache, layer, k_new, v_new, neg_f, batch, past):
    n, width = q.shape
    t = n // batch
    heads = width // FOX_HEAD_DIM
    nf = neg_f.reshape(batch * heads, 1, past + LANES)
    new = pl.BlockSpec((t, width), lambda b: (b, 0))
    old = pl.BlockSpec((past * heads, FOX_HEAD_DIM), lambda b: (layer * batch + b, 0))
    return pl.pallas_call(
        functools.partial(_fox_sample_kernel, heads=heads),
        grid=(batch,),
        in_specs=[new, old, old, new, new,
                  pl.BlockSpec((heads, 1, past + LANES), lambda b: (b, 0, 0))],
        out_specs=new,
        out_shape=jax.ShapeDtypeStruct((n, width), BF16),
        compiler_params=_cparams("parallel"),
        name="fox_attention_sample",
    )(q, k_cache, v_cache, k_new, v_new, nf)


def _merge_kernel(h_ref, a_ref, b_ref, m_ref, gates_ref, wpa_ref, wpb_ref, wpm_ref, wo_ref, o_ref):
    d = h_ref.shape[1]
    y = gates_ref[:, :d].astype(F32) * _dot(a_ref[...], wpa_ref[...])
    y = y + gates_ref[:, d:2 * d].astype(F32) * _dot(b_ref[...], wpb_ref[...])
    y = y + gates_ref[:, 2 * d:].astype(F32) * _dot(m_ref[...], wpm_ref[...])
    o_ref[...] = h_ref[...] + _dot(y.astype(BF16), wo_ref[...])


def _merge_call(h, a, b, m, gates, w_pa, w_pb, w_pm, w_o, tm):
    n, d = h.shape
    row = lambda width: pl.BlockSpec((tm, width), lambda t: (t, 0))
    const = lambda w: pl.BlockSpec(w.shape, lambda t: (0, 0))
    return pl.pallas_call(
        _merge_kernel,
        grid=(n // tm,),
        in_specs=[row(d), row(GMLP_WIDTH), row(FOX_WIDTH), row(MEM_WIDTH), row(N_BRANCH * d),
                  const(w_pa), const(w_pb), const(w_pm), const(w_o)],
        out_specs=row(d),
        out_shape=jax.ShapeDtypeStruct((n, d), F32),
        compiler_params=_cparams("parallel"),
        name="merge_branches",
    )(h, a, b, m, gates, w_pa, w_pb, w_pm, w_o)


def _top2(logits, n_exp):
    lane = lax.broadcasted_iota(jnp.int32, logits.shape, 1)
    neg = jnp.float32(-jnp.inf)
    logits = jnp.where(lane < n_exp, logits, neg)
    v1 = jnp.max(logits, axis=-1, keepdims=True)
    i1 = jnp.min(jnp.where(logits == v1, lane, LANES), axis=-1, keepdims=True)
    rest = jnp.where(lane == i1, neg, logits)
    v2 = jnp.max(rest, axis=-1, keepdims=True)
    i2 = jnp.min(jnp.where(rest == v2, lane, LANES), axis=-1, keepdims=True)
    e2 = jnp.exp(v2 - v1)
    return lane, i1, i2, 1.0 / (1.0 + e2), e2 / (1.0 + e2)


def _top2_combine(logits, n_exp):
    lane, i1, i2, p1, p2 = _top2(logits, n_exp)
    return jnp.where(lane == i1, p1, 0.0) + jnp.where(lane == i2, p2, 0.0)


def _mixer_kernel(h_ref, g_ref, *refs, routed, final_norm):
    refs = list(refs)
    wr_ref = refs.pop(0) if routed else None
    gfin_ref = refs.pop(0) if final_norm else None
    w1_ref, w3_ref, w2_ref, o_ref, xn_sc, acc_sc = refs[:6]
    comb_sc = refs[6] if routed else None
    e = pl.program_id(1)
    c = pl.program_id(2)
    first = jnp.logical_and(e == 0, c == 0)
    last = jnp.logical_and(e == pl.num_programs(1) - 1, c == pl.num_programs(2) - 1)

    @pl.when(first)
    def _():
        xn = _rms(h_ref[...], g_ref[...])
        xn_sc[...] = xn.astype(BF16)
        acc_sc[...] = jnp.zeros_like(acc_sc)
        if routed:
            logits = _dot(xn.astype(BF16), wr_ref[...])
            comb_sc[...] = _top2_combine(logits, pl.num_programs(1))

    xb = xn_sc[...]
    g1 = _dot(xb, w1_ref[0])
    g3 = _dot(xb, w3_ref[0])
    mid = (g1 * _sigmoid(g1) * g3).astype(BF16)
    y = _dot(mid, w2_ref[0])
    if routed:
        lane = lax.broadcasted_iota(jnp.int32, comb_sc.shape, 1)
        y = y * jnp.sum(jnp.where(lane == e, comb_sc[...], 0.0), axis=-1, keepdims=True)
    acc_sc[...] += y

    @pl.when(last)
    def _():
        out = h_ref[...] + acc_sc[...]
        if final_norm:
            out = _rms(out, gfin_ref[...])
        o_ref[...] = out


def _mixer_call(h, g, w1, w3, w2, tm, tf, w_router=None, g_final=None):
    n, d = h.shape
    n_exp, _, ff = w1.shape
    routed = w_router is not None
    final_norm = g_final is not None
    row = pl.BlockSpec((tm, d), lambda t, e, c: (t, 0))
    vec = pl.BlockSpec((1, d), lambda t, e, c: (0, 0))
    args, in_specs = [h, g], [row, vec]
    if routed:
        args.append(w_router)
        in_specs.append(pl.BlockSpec(w_router.shape, lambda t, e, c: (0, 0)))
    if final_norm:
        args.append(g_final)
        in_specs.append(vec)
    args += [w1, w3, w2]
    in_specs += [
        pl.BlockSpec((1, d, tf), lambda t, e, c: (e, 0, c)),
        pl.BlockSpec((1, d, tf), lambda t, e, c: (e, 0, c)),
        pl.BlockSpec((1, tf, d), lambda t, e, c: (e, c, 0)),
    ]
    scratch = [pltpu.VMEM((tm, d), BF16), pltpu.VMEM((tm, d), F32)]
    if routed:
        scratch.append(pltpu.VMEM((tm, LANES), F32))
    return pl.pallas_call(
        functools.partial(_mixer_kernel, routed=routed, final_norm=final_norm),
        grid=(n // tm, n_exp, ff // tf),
        in_specs=in_specs,
        out_specs=row,
        out_shape=jax.ShapeDtypeStruct((n, d), F32),
        scratch_shapes=scratch,
        compiler_params=_cparams("parallel", "arbitrary", "arbitrary"),
        name="channel_mixer",
    )(*args)


def _router_kernel(h_ref, g_ref, wr_ref, idx_ref, p1_ref, p2_ref, *, n_exp):
    xn = _rms(h_ref[...], g_ref[...]).astype(BF16)
    lane, i1, i2, p1, p2 = _top2(_dot(xn, wr_ref[...]), n_exp)
    idx_ref[...] = jnp.where(lane == 0, i1, jnp.where(lane == 1, i2, 0))
    p1_ref[...] = jnp.broadcast_to(p1, p1_ref.shape)
    p2_ref[...] = jnp.broadcast_to(p2, p2_ref.shape)


def _router_call(h, g, w_router, n_exp, tm):
    n, d = h.shape
    wide = pl.BlockSpec((tm, LANES), lambda t: (t, 0))
    return pl.pallas_call(
        functools.partial(_router_kernel, n_exp=n_exp),
        grid=(n // tm,),
        in_specs=[pl.BlockSpec((tm, d), lambda t: (t, 0)), pl.BlockSpec((1, d), lambda t: (0, 0)),
                  pl.BlockSpec((d, LANES), lambda t: (0, 0))],
        out_specs=[wide, wide, wide],
        out_shape=[jax.ShapeDtypeStruct((n, LANES), jnp.int32),
                   jax.ShapeDtypeStruct((n, LANES), F32),
                   jax.ShapeDtypeStruct((n, LANES), F32)],
        compiler_params=_cparams("parallel"),
        name="moe_router",
    )(h, g, w_router)


def _row_copy(src_ref, src_row, dst_ref, dst_row, sem):
    src = src_ref.at[pl.ds(pl.multiple_of(src_row * SUBLANES, SUBLANES), SUBLANES)]
    dst = dst_ref.at[pl.ds(pl.multiple_of(dst_row * SUBLANES, SUBLANES), SUBLANES)]
    return pltpu.make_async_copy(src, dst, sem)


def _slab(s, rows):
    return (pl.ds(s, rows, stride=SUBLANES), slice(None))


def _dispatch_kernel(pos1_ref, pos2_ref, h_ref, g_ref, zero_ref, xs_ref, x_sc, sem):
    del zero_ref
    tm, d = h_ref.shape
    base = pl.program_id(0) * tm
    x = _rms(h_ref[...], g_ref[...])
    for s in range(d // LANES):
        x_sc[_slab(s, tm)] = x[:, s * LANES:(s + 1) * LANES]

    def start(i, carry):
        _row_copy(x_sc, i, xs_ref, pos1_ref[base + i], sem).start(priority=0)
        _row_copy(x_sc, i, xs_ref, pos2_ref[base + i], sem).start(priority=1)
        return carry

    def wait(i, carry):
        _row_copy(x_sc, i, xs_ref, pos1_ref[base + i], sem).wait()
        _row_copy(x_sc, i, xs_ref, pos2_ref[base + i], sem).wait()
        return carry

    lax.fori_loop(0, tm, start, 0)
    lax.fori_loop(0, tm, wait, 0)


def _dispatch_call(h, g, pos1, pos2, n_rows, tm):
    n, d = h.shape
    grid_spec = pltpu.PrefetchScalarGridSpec(
        num_scalar_prefetch=2,
        grid=(n // tm,),
        in_specs=[pl.BlockSpec((tm, d), lambda t, p1, p2: (t, 0)),
                  pl.BlockSpec((1, d), lambda t, p1, p2: (0, 0)),
                  pl.BlockSpec(memory_space=pl.ANY)],
        out_specs=pl.BlockSpec(memory_space=pl.ANY),
        scratch_shapes=[pltpu.VMEM((tm * SUBLANES, LANES), F32), pltpu.SemaphoreType.DMA(())],
    )
    assert d == SUBLANES * LANES
    return pl.pallas_call(
        _dispatch_kernel,
        grid_spec=grid_spec,
        out_shape=jax.ShapeDtypeStruct((n_rows * SUBLANES, LANES), F32),
        input_output_aliases={4: 0},
        compiler_params=_cparams("arbitrary"),
        name="moe_dispatch",
    )(pos1, pos2, h, g, jnp.zeros((n_rows * SUBLANES, LANES), F32))


def _experts_kernel(te_ref, nact_ref, x_ref, w1_ref, w3_ref, w2_ref, y_ref, xb_sc, acc_sc):
    t = pl.program_id(0)
    c = pl.program_id(1)

    tm, d = acc_sc.shape

    @pl.when(c == 0)
    def _():
        for s in range(d // LANES):
            xb_sc[:, s * LANES:(s + 1) * LANES] = x_ref[_slab(s, tm)].astype(BF16)
        acc_sc[...] = jnp.zeros_like(acc_sc)

    @pl.when(t < nact_ref[0])
    def _():
        xb = xb_sc[...]
        g1 = _dot(xb, w1_ref[0])
        g3 = _dot(xb, w3_ref[0])
        mid = (g1 * _sigmoid(g1) * g3).astype(BF16)
        acc_sc[...] += _dot(mid, w2_ref[0])

    @pl.when(c == pl.num_programs(1) - 1)
    def _():
        for s in range(d // LANES):
            y_ref[_slab(s, tm)] = acc_sc[:, s * LANES:(s + 1) * LANES]


def _experts_call(xs, tile_expert, n_active, w1, w3, w2, tm, tf):
    n_rows = xs.shape[0] // SUBLANES
    d = w1.shape[1]
    ff = w1.shape[2]
    nff = ff // tf

    def chunk(t, c, te, nact):
        return jnp.where(t < nact[0], c, nff - 1)

    grid_spec = pltpu.PrefetchScalarGridSpec(
        num_scalar_prefetch=2,
        grid=(n_rows // tm, nff),
        in_specs=[
            pl.BlockSpec((tm * SUBLANES, LANES), lambda t, c, te, nact: (t, 0)),
            pl.BlockSpec((1, d, tf), lambda t, c, te, nact: (te[t], 0, chunk(t, c, te, nact))),
            pl.BlockSpec((1, d, tf), lambda t, c, te, nact: (te[t], 0, chunk(t, c, te, nact))),
            pl.BlockSpec((1, tf, d), lambda t, c, te, nact: (te[t], chunk(t, c, te, nact), 0)),
        ],
        out_specs=pl.BlockSpec((tm * SUBLANES, LANES), lambda t, c, te, nact: (t, 0)),
        scratch_shapes=[pltpu.VMEM((tm, d), BF16), pltpu.VMEM((tm, d), F32)],
    )
    return pl.pallas_call(
        _experts_kernel,
        grid_spec=grid_spec,
        out_shape=jax.ShapeDtypeStruct((n_rows * SUBLANES, LANES), F32),
        compiler_params=_cparams("parallel", "arbitrary"),
        name="moe_experts",
    )(tile_expert, n_active, xs, w1, w3, w2)


def _combine_kernel(pos1_ref, pos2_ref, h_ref, p1_ref, p2_ref, *refs, final_norm):
    refs = list(refs)
    gfin_ref = refs.pop(0) if final_norm else None
    ys_ref, o_ref, y1_sc, y2_sc, sem = refs
    tm, d = h_ref.shape
    base = pl.program_id(0) * tm

    def start(i, carry):
        _row_copy(ys_ref, pos1_ref[base + i], y1_sc, i, sem).start(priority=0)
        _row_copy(ys_ref, pos2_ref[base + i], y2_sc, i, sem).start(priority=1)
        return carry

    def wait(i, carry):
        _row_copy(ys_ref, pos1_ref[base + i], y1_sc, i, sem).wait()
        _row_copy(ys_ref, pos2_ref[base + i], y2_sc, i, sem).wait()
        return carry

    lax.fori_loop(0, tm, start, 0)
    lax.fori_loop(0, tm, wait, 0)
    w1 = p1_ref[...]
    w2 = p2_ref[...]
    out = jnp.concatenate(
        [h_ref[:, s * LANES:(s + 1) * LANES]
         + (w1 * y1_sc[_slab(s, tm)] + w2 * y2_sc[_slab(s, tm)]) for s in range(d // LANES)], axis=1)
    if final_norm:
        out = _rms(out, gfin_ref[...])
    o_ref[...] = out


def _combine_call(h, p1, p2, ys, pos1, pos2, tm, g_final=None):
    n, d = h.shape
    final_norm = g_final is not None
    row = pl.BlockSpec((tm, d), lambda t, a, b: (t, 0))
    wide = pl.BlockSpec((tm, LANES), lambda t, a, b: (t, 0))
    args, in_specs = [h, p1, p2], [row, wide, wide]
    if final_norm:
        args.append(g_final)
        in_specs.append(pl.BlockSpec((1, d), lambda t, a, b: (0, 0)))
    args.append(ys)
    in_specs.append(pl.BlockSpec(memory_space=pl.ANY))
    grid_spec = pltpu.PrefetchScalarGridSpec(
        num_scalar_prefetch=2,
        grid=(n // tm,),
        in_specs=in_specs,
        out_specs=row,
        scratch_shapes=[pltpu.VMEM((tm * SUBLANES, LANES), F32),
                        pltpu.VMEM((tm * SUBLANES, LANES), F32),
                        pltpu.SemaphoreType.DMA(())],
    )
    return pl.pallas_call(
        functools.partial(_combine_kernel, final_norm=final_norm),
        grid_spec=grid_spec,
        out_shape=jax.ShapeDtypeStruct((n, d), F32),
        compiler_params=_cparams("arbitrary"),
        name="moe_combine",
    )(pos1, pos2, *args)


def _routing_tables(idx, n_exp, tm):
    n = idx.shape[0]
    e = idx[:, :TOP_K].reshape(-1)
    onehot = (e[:, None] == jnp.arange(n_exp, dtype=jnp.int32)[None, :]).astype(jnp.int32)
    csum = jnp.cumsum(onehot, axis=0)
    rank = jnp.take_along_axis(csum, e[:, None], axis=1)[:, 0] - 1
    counts = csum[-1]
    padded = (counts + tm - 1) // tm * tm
    seg_end = jnp.cumsum(padded)
    pos = ((seg_end - padded)[e] + rank).reshape(n, TOP_K)
    n_rows = TOP_K * n + n_exp * tm
    tile_first = jnp.arange(n_rows // tm, dtype=jnp.int32) * tm
    tile_expert = jnp.minimum(jnp.sum(tile_first[:, None] >= seg_end[None, :], axis=1), n_exp - 1)
    n_active = (seg_end[-1] // tm).reshape(1)
    return pos[:, 0], pos[:, 1], tile_expert.astype(jnp.int32), n_active.astype(jnp.int32), n_rows


def _routed_moe(h, g, w_router, w1, w3, w2, tm_tok, tm_rows, tf, g_final):
    n_exp = w1.shape[0]
    idx, p1, p2 = _router_call(h, g, w_router, n_exp, tm_tok)
    pos1, pos2, tile_expert, n_active, n_rows = _routing_tables(idx, n_exp, tm_rows)
    xs = _dispatch_call(h, g, pos1, pos2, n_rows, tm_tok)
    ys = _experts_call(xs, tile_expert, n_active, w1, w3, w2, tm_rows, tf)
    return _combine_call(h, p1, p2, ys, pos1, pos2, tm_tok, g_final)


def _largest_tile(n, cap, quantum):
    best = quantum
    t = quantum
    while t <= min(n, cap):
        if n % t == 0:
            best = t
        t += quantum
    return best


def _spatial_operands(w_s, b_s, blk):
    reps = GMLP_BLOCK // blk
    w = w_s[:, :blk, :blk] * jnp.tril(jnp.ones((blk, blk), w_s.dtype))
    eye = jnp.eye(reps, dtype=w_s.dtype)
    w = jnp.einsum("ab,gts->gatbs", eye, w).reshape(GMLP_GROUPS, GMLP_BLOCK, GMLP_BLOCK)
    b = jnp.tile(b_s[:, :blk], (1, reps))
    ws_pairs = jnp.concatenate([w[0::2], w[1::2]], axis=2).astype(BF16)
    bs_pairs = jnp.concatenate(
        [jnp.broadcast_to(b[0::2, :, None], (GMLP_GROUPS // 2, GMLP_BLOCK, GMLP_GROUP_DIM)),
         jnp.broadcast_to(b[1::2, :, None], (GMLP_GROUPS // 2, GMLP_BLOCK, GMLP_GROUP_DIM))],
        axis=2).astype(F32)
    return ws_pairs, bs_pairs


def kernel(x_prompt, x_sample, cache_fox_k, cache_fox_v, cache_fox_logf, cache_mem_k, cache_mem_v, mem_prompt, g_mix, w_in, b_f, ln_v_g, ln_v_b, w_s, b_s, g_mem, w_mem_kv, w_pa, w_pb, w_pm, w_o, g_ffn, w1, w3, w2, w_router, e_w1, e_w3, e_w2, g_final):
    depth = w_in.shape[0]
    bp, sp, d = x_prompt.shape
    bs, ts, _ = x_sample.shape
    past = cache_fox_k.shape[2]
    n_mem = mem_prompt.shape[1]
    n_exp = e_w1.shape[1]
    assert sp % GMLP_BLOCK == 0 and GMLP_BLOCK % ts == 0 and (bs * ts) % GMLP_BLOCK == 0

    off_q = 2 * GMLP_WIDTH
    off_f = off_q + 3 * FOX_WIDTH
    off_mq = off_f + FOX_HEADS
    off_g = off_mq + MEM_WIDTH

    hp = x_prompt.reshape(bp * sp, d)
    hs = x_sample.reshape(bs * ts, d)
    mem = mem_prompt.reshape(bp * n_mem, d)
    cache_k_rows = cache_fox_k.reshape(depth * bs * past * FOX_HEADS, FOX_HEAD_DIM)
    cache_v_rows = cache_fox_v.reshape(depth * bs * past * FOX_HEADS, FOX_HEAD_DIM)

    tm_p = _largest_tile(sp, 512, GMLP_BLOCK)
    tm_s = GMLP_BLOCK
    fox_group = 2 * LANES
    fox_heads = 4
    fox_tile = _largest_tile(sp, 1024, fox_group)
    tmix_p = _largest_tile(bp * sp, 1024, LANES)
    tmix_s = bs * ts

    outs = {k: [] for k in ("fk_p", "fv_p", "fl_p", "mk_p", "mv_p", "fk_s", "fv_s", "fl_s", "gv_s")}
    for i in range(depth):
        g = g_mix[i].reshape(1, d)
        wi = w_in[i]
        w_uv = wi[:, :off_q].astype(BF16)
        w_qkv = wi[:, off_q:off_f].astype(BF16)
        w_f = jnp.zeros((d, LANES), BF16).at[:, :FOX_HEADS].set(wi[:, off_f:off_mq].astype(BF16))
        w_mq = wi[:, off_mq:off_g].astype(BF16)
        w_g = wi[:, off_g:].astype(BF16)
        bfi = jnp.zeros((1, LANES), F32).at[0, :FOX_HEADS].set(b_f[i])
        lng = ln_v_g[i].reshape(1, GMLP_WIDTH)
        lnb = ln_v_b[i].reshape(1, GMLP_WIDTH)
        wpa, wpb, wpm, wo = (w[i].astype(BF16) for w in (w_pa, w_pb, w_pm, w_o))
        gf = g_ffn[i].reshape(1, d)
        j = i // 2
        if i % 2 == 0:
            mw1, mw3, mw2 = (w[j][None].astype(BF16) for w in (w1, w3, w2))
            wr = None
        else:
            mw1, mw3, mw2 = (w[j].astype(BF16) for w in (e_w1, e_w3, e_w2))
            wr = jnp.zeros((d, LANES), BF16).at[:, :n_exp].set(w_router[j].astype(BF16))
        ff = mw1.shape[2]
        tf = _largest_tile(ff, 1408, LANES)
        gfin = g_final.reshape(1, d) if i == depth - 1 else None

        ws_pairs, bs_pairs = _spatial_operands(w_s[i], b_s[i], GMLP_BLOCK)
        mk, mv = _memkv_call(mem, g_mem[i].reshape(1, d), w_mem_kv[i].astype(BF16), n_mem)
        a, _ = _uv_call(hp, g, w_uv, lng, lnb, ws_pairs, bs_pairs, tm_p)
        q_t, k, v, kb, v_t = _qkv_call(hp, g, w_qkv, tm_p, True)
        logf, m, gates, kf = _fmg_call(hp, g, w_f, bfi, w_mq, w_g, mk, mv, bp, tm_p, BF16, True)
        b = _fox_call(q_t, kb, kf, v_t, bp, fox_tile, fox_group, fox_heads)
        hp = _merge_call(hp, a, b, m, gates, wpa, wpb, wpm, wo, tm_p)
        if wr is None:
            hp = _mixer_call(hp, gf, mw1, mw3, mw2, tmix_p, tf, None, gfin)
        else:
            hp = _routed_moe(hp, gf, wr, mw1, mw3, mw2, tm_p, tmix_p, tf, gfin)
        outs["fk_p"].append(k.reshape(bp, sp, FOX_HEADS, FOX_HEAD_DIM))
        outs["fv_p"].append(v.reshape(bp, sp, FOX_HEADS, FOX_HEAD_DIM))
        outs["fl_p"].append(logf.reshape(bp, sp, FOX_HEADS))
        outs["mk_p"].append(mk.reshape(bp, n_mem, MEM_HEADS, MEM_HEAD_DIM))
        outs["mv_p"].append(mv.reshape(bp, n_mem, MEM_HEADS, MEM_HEAD_DIM))

        ws_pairs, bs_pairs = _spatial_operands(w_s[i], b_s[i], ts)
        a, vn = _uv_call(hs, g, w_uv, lng, lnb, ws_pairs, bs_pairs, tm_s)
        q, k, v = _qkv_call(hs, g, w_qkv, bs * ts, False)
        logf, m, gates = _fmg_call(hs, g, w_f, bfi, w_mq, w_g,
                                   cache_mem_k[i].reshape(bs * n_mem, MEM_WIDTH),
                                   cache_mem_v[i].reshape(bs * n_mem, MEM_WIDTH), bs, ts, F32, False)
        logf = logf.reshape(bs, ts, FOX_HEADS)
        logf_all = jnp.concatenate(
            [cache_fox_logf[i].transpose(0, 2, 1), logf.transpose(0, 2, 1),
             jnp.zeros((bs, FOX_HEADS, LANES - ts), F32)], axis=2)
        neg_f = _neg_cumsum_call(logf_all.reshape(bs * FOX_HEADS, past + LANES))
        b = _fox_sample_call(q, cache_k_rows, cache_v_rows, i, k, v, neg_f, bs, past)
        hs = _merge_call(hs, a, b, m, gates, wpa, wpb, wpm, wo, tmix_s)
        hs = _mixer_call(hs, gf, mw1, mw3, mw2, tmix_s, tf, wr, gfin)
        outs["fk_s"].append(k.reshape(bs, ts, FOX_HEADS, FOX_HEAD_DIM))
        outs["fv_s"].append(v.reshape(bs, ts, FOX_HEADS, FOX_HEAD_DIM))
        outs["fl_s"].append(logf)
        outs["gv_s"].append(vn.reshape(bs, ts, GMLP_WIDTH))

    st = {k: jnp.stack(v) for k, v in outs.items()}
    return (hp.reshape(bp, sp, d), hs.reshape(bs, ts, d),
            st["fk_p"], st["fv_p"], st["fl_p"], st["mk_p"], st["mv_p"],
            st["fk_s"], st["fv_s"], st["fl_s"], st["gv_s"])
```

```python
import functools

import numpy as np
import jax
import jax.numpy as jnp
from jax import lax
from jax.experimental import pallas as pl
from jax.experimental.pallas import tpu as pltpu

F32 = jnp.float32
BF16 = jnp.bfloat16

EPS = 1e-6
GMLP_GROUPS = 8
GMLP_GROUP_DIM = 64
GMLP_WIDTH = GMLP_GROUPS * GMLP_GROUP_DIM
GMLP_BLOCK = 128
FOX_HEADS = 8
FOX_HEAD_DIM = 128
FOX_WIDTH = FOX_HEADS * FOX_HEAD_DIM
MEM_HEADS = 4
MEM_HEAD_DIM = 128
MEM_WIDTH = MEM_HEADS * MEM_HEAD_DIM
N_BRANCH = 3
TOP_K = 2
LANES = 128
SUBLANES = 8
BF16_ROWS = 16
LOG2E = 1.4426950408889634
VMEM_LIMIT = 56 * 1024 * 1024

_NT = (((1,), (1,)), ((), ()))


def _cparams(*sem):
    return pltpu.CompilerParams(dimension_semantics=sem, vmem_limit_bytes=VMEM_LIMIT)


def _rms(x, g):
    return x * lax.rsqrt(jnp.mean(x * x, axis=-1, keepdims=True) + EPS) * g


def _sigmoid(x):
    return 1.0 / (1.0 + jnp.exp(-x))


def _dot(a, b):
    return jnp.dot(a, b, preferred_element_type=F32)


def _dot_nt(a, b):
    return lax.dot_general(a, b, _NT, preferred_element_type=F32)


def _uv_kernel(h_ref, g_ref, w_ref, lng_ref, lnb_ref, ws_ref, bs_ref, a_ref, vn_ref):
    tm = h_ref.shape[0]
    xn = _rms(h_ref[...], g_ref[...]).astype(BF16)
    z = _dot(xn, w_ref[...])
    uv = 0.5 * z * (1.0 + lax.erf(z * np.float32(2.0 ** -0.5)))
    u = uv[:, :GMLP_WIDTH]
    v = uv[:, GMLP_WIDTH:]
    mu = jnp.mean(v, axis=-1, keepdims=True)
    vc = v - mu
    var = jnp.mean(vc * vc, axis=-1, keepdims=True)
    vn = vc * lax.rsqrt(var + EPS) * lng_ref[...] + lnb_ref[...]
    vn_ref[...] = vn
    lane = lax.broadcasted_iota(jnp.int32, (GMLP_BLOCK, LANES), 1)
    low = lane < GMLP_GROUP_DIM
    for r in range(tm // GMLP_BLOCK):
        rows = slice(r * GMLP_BLOCK, (r + 1) * GMLP_BLOCK)
        for p in range(GMLP_WIDTH // LANES):
            cols = slice(p * LANES, (p + 1) * LANES)
            vs = vn[rows, cols].astype(BF16)
            zero = jnp.zeros_like(vs)
            rhs = jnp.concatenate([jnp.where(low, vs, zero), jnp.where(low, zero, vs)], axis=0)
            mixed = _dot(ws_ref[p], rhs) + bs_ref[p]
            a_ref[rows, cols] = (u[rows, cols] * mixed).astype(a_ref.dtype)


def _uv_call(h, g, w_uv, ln_g, ln_b, ws_pairs, bs_pairs, tm):
    n, d = h.shape
    const = lambda *shape: pl.BlockSpec(shape, lambda t: (0,) * len(shape))
    return pl.pallas_call(
        _uv_kernel,
        grid=(n // tm,),
        in_specs=[
            pl.BlockSpec((tm, d), lambda t: (t, 0)),
            const(1, d),
            const(d, 2 * GMLP_WIDTH),
            const(1, GMLP_WIDTH),
            const(1, GMLP_WIDTH),
            const(*ws_pairs.shape),
            const(*bs_pairs.shape),
        ],
        out_specs=[
            pl.BlockSpec((tm, GMLP_WIDTH), lambda t: (t, 0)),
            pl.BlockSpec((tm, GMLP_WIDTH), lambda t: (t, 0)),
        ],
        out_shape=[
            jax.ShapeDtypeStruct((n, GMLP_WIDTH), BF16),
            jax.ShapeDtypeStruct((n, GMLP_WIDTH), F32),
        ],
        compiler_params=_cparams("parallel"),
        name="inproj_gmlp",
    )(h, g, w_uv, ln_g, ln_b, ws_pairs, bs_pairs)


def _qkv_kernel(h_ref, g_ref, w_ref, q_ref, k_ref, v_ref, *bf_refs, for_flash):
    xn = _rms(h_ref[...], g_ref[...]).astype(BF16)
    q = _dot(xn, w_ref[:, :FOX_WIDTH])
    k = _dot(xn, w_ref[:, FOX_WIDTH:2 * FOX_WIDTH])
    v = _dot(xn, w_ref[:, 2 * FOX_WIDTH:])
    k_ref[...] = k
    v_ref[...] = v
    if for_flash:
        kb_ref, vt_ref = bf_refs
        q_ref[...] = (q * np.float32(FOX_HEAD_DIM ** -0.5 * LOG2E)).T.astype(BF16)
        kb_ref[...] = k.astype(BF16)
        vt_ref[...] = v.T.astype(BF16)
    else:
        q_ref[...] = q.astype(BF16)


def _qkv_call(h, g, w_qkv, tm, for_flash):
    n, d = h.shape
    row = lambda width: pl.BlockSpec((tm, width), lambda t: (t, 0))
    col = pl.BlockSpec((FOX_WIDTH, tm), lambda t: (0, t))
    rows_f32 = jax.ShapeDtypeStruct((n, FOX_WIDTH), F32)
    rows_bf16 = jax.ShapeDtypeStruct((n, FOX_WIDTH), BF16)
    cols_bf16 = jax.ShapeDtypeStruct((FOX_WIDTH, n), BF16)
    if for_flash:
        out_specs = [col, row(FOX_WIDTH), row(FOX_WIDTH), row(FOX_WIDTH), col]
        out_shape = [cols_bf16, rows_f32, rows_f32, rows_bf16, cols_bf16]
    else:
        out_specs = [row(FOX_WIDTH)] * 3
        out_shape = [rows_bf16, rows_f32, rows_f32]
    return pl.pallas_call(
        functools.partial(_qkv_kernel, for_flash=for_flash),
        grid=(n // tm,),
        in_specs=[row(d), pl.BlockSpec((1, d), lambda t: (0, 0)),
                  pl.BlockSpec((d, 3 * FOX_WIDTH), lambda t: (0, 0))],
        out_specs=out_specs,
        out_shape=out_shape,
        compiler_params=_cparams("parallel"),
        name="inproj_qkv",
    )(h, g, w_qkv)


def _split3(x):
    hi = x.astype(BF16)
    r = x - hi.astype(F32)
    mid = r.astype(BF16)
    lo = (r - mid.astype(F32)).astype(BF16)
    return hi, mid, lo


def _fmg_kernel(h_ref, g_ref, wf_ref, bf_ref, wmq_ref, wg_ref, mk_ref, mv_ref, *rest,
                tiles_per_seq, bias_cols):
    if bias_cols:
        tri_ref, sel_ref, logf_ref, m_ref, gates_ref, kf_ref, carry_sc = rest
    else:
        logf_ref, m_ref, gates_ref = rest
    tm = h_ref.shape[0]
    xn = _rms(h_ref[...], g_ref[...]).astype(BF16)
    zf = _dot(xn, wf_ref[...]) + bf_ref[...]
    logf = jnp.minimum(zf, 0.0) - jnp.log1p(jnp.exp(-jnp.abs(zf)))
    logf_ref[...] = logf[:, :FOX_HEADS]
    if bias_cols:
        @pl.when(pl.program_id(0) % tiles_per_seq == 0)
        def _():
            carry_sc[...] = jnp.zeros_like(carry_sc)

        lane = lax.broadcasted_iota(jnp.int32, logf.shape, 1)
        x = jnp.where(lane < FOX_HEADS, logf, 0.0)
        r = _dot(tri_ref[...], jnp.concatenate(_split3(x), axis=1))
        f = r[:, :LANES] + r[:, LANES:2 * LANES] + r[:, 2 * LANES:] + carry_sc[...]
        carry_sc[...] = f[tm - 1:tm, :]
        parts = jnp.concatenate(_split3(f * np.float32(-LOG2E)), axis=1)
        kf_ref[...] = _dot(parts, sel_ref[...]).astype(BF16)
    mq = _dot(xn, wmq_ref[...]).astype(BF16)
    for hh in range(MEM_HEADS):
        cols = slice(hh * MEM_HEAD_DIM, (hh + 1) * MEM_HEAD_DIM)
        s = _dot_nt(mq[:, cols], mk_ref[:, cols].astype(BF16)) * np.float32(MEM_HEAD_DIM ** -0.5)
        p = jnp.exp(s - jnp.max(s, axis=-1, keepdims=True))
        p = (p / jnp.sum(p, axis=-1, keepdims=True)).astype(BF16)
        m_ref[:, cols] = _dot(p, mv_ref[:, cols].astype(BF16)).astype(m_ref.dtype)
    gates_ref[...] = _sigmoid(_dot(xn, wg_ref[...])).astype(gates_ref.dtype)


def _fmg_call(h, g, w_f, b_f, w_mq, w_g, mk, mv, batch, tm, gates_dtype, bias_cols):
    n, d = h.shape
    seq = n // batch
    tpb = seq // tm
    n_mem = mk.shape[0] // batch
    const = lambda *shape: pl.BlockSpec(shape, lambda t: (0,) * len(shape))
    row = lambda width: pl.BlockSpec((tm, width), lambda t: (t, 0))
    args = [h, g, w_f, b_f, w_mq, w_g, mk, mv]
    in_specs = [
        row(d), const(1, d), const(d, LANES), const(1, LANES), const(d, MEM_WIDTH),
        const(d, N_BRANCH * d),
        pl.BlockSpec((n_mem, MEM_WIDTH), lambda t: (t // tpb, 0)),
        pl.BlockSpec((n_mem, MEM_WIDTH), lambda t: (t // tpb, 0)),
    ]
    out_specs = [row(FOX_HEADS), row(MEM_WIDTH), row(N_BRANCH * d)]
    out_shape = [
        jax.ShapeDtypeStruct((n, FOX_HEADS), F32),
        jax.ShapeDtypeStruct((n, MEM_WIDTH), BF16),
        jax.ShapeDtypeStruct((n, N_BRANCH * d), gates_dtype),
    ]
    scratch = []
    if bias_cols:
        tri = jnp.tril(jnp.ones((tm, tm), BF16))
        head = jnp.arange(FOX_HEADS)
        sel = jnp.zeros((3 * LANES, FOX_WIDTH), BF16)
        for c in range(3):
            sel = sel.at[c * LANES + head, head * FOX_HEAD_DIM + c].set(1)
        args += [tri, sel]
        in_specs += [const(tm, tm), const(3 * LANES, FOX_WIDTH)]
        out_specs.append(row(FOX_WIDTH))
        out_shape.append(jax.ShapeDtypeStruct((n, FOX_WIDTH), BF16))
        scratch.append(pltpu.VMEM((1, LANES), F32))
    return pl.pallas_call(
        functools.partial(_fmg_kernel, tiles_per_seq=tpb, bias_cols=bias_cols),
        grid=(n // tm,),
        in_specs=in_specs,
        out_specs=out_specs,
        out_shape=out_shape,
        scratch_shapes=scratch,
        compiler_params=_cparams("arbitrary" if bias_cols else "parallel"),
        name="inproj_logf_mem_gates",
    )(*args)


def _memkv_kernel(x_ref, g_ref, w_ref, mk_ref, mv_ref):
    xn = _rms(x_ref[...], g_ref[...]).astype(BF16)
    mk_ref[...] = _dot(xn, w_ref[:, :MEM_WIDTH])
    mv_ref[...] = _dot(xn, w_ref[:, MEM_WIDTH:])


def _memkv_call(mem, g, w_kv, tm):
    n, d = mem.shape
    return pl.pallas_call(
        _memkv_kernel,
        grid=(n // tm,),
        in_specs=[pl.BlockSpec((tm, d), lambda t: (t, 0)),
                  pl.BlockSpec((1, d), lambda t: (0, 0)),
                  pl.BlockSpec((d, 2 * MEM_WIDTH), lambda t: (0, 0))],
        out_specs=[pl.BlockSpec((tm, MEM_WIDTH), lambda t: (t, 0))] * 2,
        out_shape=[jax.ShapeDtypeStruct((n, MEM_WIDTH), F32)] * 2,
        compiler_params=_cparams("parallel"),
        name="memory_kv",
    )(mem, g, w_kv)


def _neg_cumsum_kernel(x_ref, o_ref):
    x = x_ref[...]
    width = x.shape[-1]
    lane = lax.broadcasted_iota(jnp.int32, x.shape, 1)
    d = 1
    while d < width:
        x = x + jnp.where(lane >= d, pltpu.roll(x, d, 1), 0.0)
        d *= 2
    o_ref[...] = -x


def _neg_cumsum_call(x):
    return pl.pallas_call(
        _neg_cumsum_kernel,
        out_shape=jax.ShapeDtypeStruct(x.shape, F32),
        compiler_params=pltpu.CompilerParams(vmem_limit_bytes=VMEM_LIMIT),
        name="logf_cumsum",
    )(x)


def _fox_kernel(qi_ref, kj_ref, flag_ref, qt_ref, ones_ref, k_ref, kf_ref, vt_ref, o_ref,
                m_sc, acc_sc, *, tile, group, heads):
    step = pl.program_id(2)
    flags = flag_ref[step]
    acc_rows = FOX_HEAD_DIM + BF16_ROWS

    @pl.when(flags & 1 != 0)
    def _():
        m_sc[...] = jnp.full_like(m_sc, -jnp.inf)
        acc_sc[...] = jnp.zeros_like(acc_sc)

    def update(diagonal):
        n_groups = tile // group
        scores = []
        for hd in range(heads):
            hs = slice(hd * FOX_HEAD_DIM, (hd + 1) * FOX_HEAD_DIM)
            ka = jnp.concatenate([k_ref[:, hs], kf_ref[:, hs]], axis=1)
            for gi in range(n_groups):
                cols = slice(gi * group, (gi + 1) * group)
                rows = (gi + 1) * group if diagonal else tile
                qa = jnp.concatenate([qt_ref[hs, cols], ones_ref[:, :group]], axis=0)
                s = _dot(ka[:rows], qa)
                if diagonal:
                    key = lax.broadcasted_iota(jnp.int32, s.shape, 0)
                    qry = lax.broadcasted_iota(jnp.int32, s.shape, 1) + gi * group
                    s = jnp.where(key <= qry, s, -jnp.inf)
                scores.append(s)
        for hd in range(heads):
            hs = slice(hd * FOX_HEAD_DIM, (hd + 1) * FOX_HEAD_DIM)
            ar = slice(hd * acc_rows, (hd + 1) * acc_rows)
            va = jnp.concatenate([vt_ref[hs, :], ones_ref[:BF16_ROWS]], axis=0)
            for gi in range(n_groups):
                s = scores[hd * n_groups + gi]
                cols = slice(gi * group, (gi + 1) * group)
                rows = s.shape[0]
                m_prev = m_sc[hd:hd + 1, cols]
                m_new = jnp.maximum(m_prev, jnp.max(s, axis=0, keepdims=True))
                p = jnp.exp2(s - m_new).astype(BF16)
                acc_sc[ar, cols] = (jnp.exp2(m_prev - m_new) * acc_sc[ar, cols]
                                    + _dot(va[:, :rows], p))
                m_sc[hd:hd + 1, cols] = m_new

    @pl.when(flags & 2 == 0)
    def _():
        update(False)

    @pl.when(flags & 2 != 0)
    def _():
        update(True)
        for hd in range(heads):
            acc = acc_sc[hd * acc_rows:(hd + 1) * acc_rows]
            out = acc[:FOX_HEAD_DIM] * (1.0 / acc[FOX_HEAD_DIM:FOX_HEAD_DIM + 1])
            o_ref[:, hd * FOX_HEAD_DIM:(hd + 1) * FOX_HEAD_DIM] = out.T.astype(o_ref.dtype)


def _fox_tables(n_tiles):
    qi, kj, flags = [], [], []
    for i in range(n_tiles):
        for j in range(i + 1):
            qi.append(i)
            kj.append(j)
            flags.append((1 if j == 0 else 0) | (2 if j == i else 0))
    return (np.asarray(qi, np.int32), np.asarray(kj, np.int32), np.asarray(flags, np.int32))


def _fox_call(q_t, k, kf, v_t, batch, tile, group, heads_per_step):
    width, n = q_t.shape
    seq = n // batch
    hw = heads_per_step * FOX_HEAD_DIM
    nt = seq // tile
    qi, kj, flags = _fox_tables(nt)
    ones = jnp.zeros((FOX_HEAD_DIM, tile), BF16).at[:3].set(1)
    q_map = lambda b, h, s, qi, kj, fl: (h, b * nt + qi[s])
    kt_map = lambda b, h, s, qi, kj, fl: (h, b * nt + kj[s])
    k_map = lambda b, h, s, qi, kj, fl: (b * nt + kj[s], h)
    grid_spec = pltpu.PrefetchScalarGridSpec(
        num_scalar_prefetch=3,
        grid=(batch, width // hw, len(qi)),
        in_specs=[
            pl.BlockSpec((hw, tile), q_map),
            pl.BlockSpec((FOX_HEAD_DIM, tile), lambda b, h, s, qi, kj, fl: (0, 0)),
            pl.BlockSpec((tile, hw), k_map),
            pl.BlockSpec((tile, hw), k_map),
            pl.BlockSpec((hw, tile), kt_map),
        ],
        out_specs=pl.BlockSpec((tile, hw), lambda b, h, s, qi, kj, fl: (b * nt + qi[s], h)),
        scratch_shapes=[
            pltpu.VMEM((heads_per_step, tile), F32),
            pltpu.VMEM((heads_per_step * (FOX_HEAD_DIM + BF16_ROWS), tile), F32),
        ],
    )
    return pl.pallas_call(
        functools.partial(_fox_kernel, tile=tile, group=group, heads=heads_per_step),
        grid_spec=grid_spec,
        out_shape=jax.ShapeDtypeStruct((n, width), BF16),
        compiler_params=_cparams("parallel", "parallel", "arbitrary"),
        name="fox_attention",
    )(jnp.asarray(qi), jnp.asarray(kj), jnp.asarray(flags), q_t, ones, k, kf, v_t)


def _fox_sample_kernel(q_ref, kc_ref, vc_ref, kn_ref, vn_ref, nf_ref, o_ref, *, heads):
    t = q_ref.shape[0]
    past = kc_ref.shape[0] // heads
    scale = np.float32(FOX_HEAD_DIM ** -0.5)
    pad = jnp.zeros((LANES - t, FOX_HEAD_DIM), BF16)
    row = lax.broadcasted_iota(jnp.int32, (t, LANES), 0)
    col = lax.broadcasted_iota(jnp.int32, (t, LANES), 1)
    for hd in range(heads):
        hs = slice(hd * FOX_HEAD_DIM, (hd + 1) * FOX_HEAD_DIM)
        cached = (pl.ds(hd, past, stride=heads), slice(None))
        q = q_ref[:, hs]
        nf = nf_ref[hd]
        kn = jnp.concatenate([kn_ref[:, hs].astype(BF16), pad], axis=0)
        vn = jnp.concatenate([vn_ref[:, hs].astype(BF16), pad], axis=0)
        s_c = _dot_nt(q, kc_ref[cached].astype(BF16)) * scale + nf[:, :past]
        s_n = _dot_nt(q, kn) * scale + nf[:, past:]
        s_n = jnp.where(col <= row, s_n, -jnp.inf)
        m = jnp.maximum(jnp.max(s_c, axis=-1, keepdims=True), jnp.max(s_n, axis=-1, keepdims=True))
        p_c = jnp.exp(s_c - m)
        p_n = jnp.exp(s_n - m)
        l = jnp.sum(p_c, axis=-1, keepdims=True) + jnp.sum(p_n, axis=-1, keepdims=True)
        p_c = (p_c / l).astype(BF16)
        p_n = (p_n / l).astype(BF16)
        o_ref[:, hs] = (_dot(p_c, vc_ref[cached].astype(BF16)) + _dot(p_n, vn)).astype(o_ref.dtype)


def _fox_sample_call(q, k_cache, v_cache, layer, k_new, v_new, neg_f, batch, past):
    n, width = q.shape
    t = n // batch
    heads = width // FOX_HEAD_DIM
    nf = neg_f.reshape(batch * heads, 1, past + LANES)
    new = pl.BlockSpec((t, width), lambda b: (b, 0))
    old = pl.BlockSpec((past * heads, FOX_HEAD_DIM), lambda b: (layer * batch + b, 0))
    return pl.pallas_call(
        functools.partial(_fox_sample_kernel, heads=heads),
        grid=(batch,),
        in_specs=[new, old, old, new, new,
                  pl.BlockSpec((heads, 1, past + LANES), lambda b: (b, 0, 0))],
        out_specs=new,
        out_shape=jax.ShapeDtypeStruct((n, width), BF16),
        compiler_params=_cparams("parallel"),
        name="fox_attention_sample",
    )(q, k_cache, v_cache, k_new, v_new, nf)


def _merge_kernel(h_ref, a_ref, b_ref, m_ref, gates_ref, wpa_ref, wpb_ref, wpm_ref, wo_ref, *rest,
                  n_exp):
    d = h_ref.shape[1]
    y = gates_ref[:, :d].astype(F32) * _dot(a_ref[...], wpa_ref[...])
    y = y + gates_ref[:, d:2 * d].astype(F32) * _dot(b_ref[...], wpb_ref[...])
    y = y + gates_ref[:, 2 * d:].astype(F32) * _dot(m_ref[...], wpm_ref[...])
    out = h_ref[...] + _dot(y.astype(BF16), wo_ref[...])
    if n_exp:
        gf_ref, wr_ref, o_ref, idx_ref, p1_ref, p2_ref = rest
        xn = _rms(out, gf_ref[...]).astype(BF16)
        lane, i1, i2, p1, p2 = _top2(_dot(xn, wr_ref[...]), n_exp)
        idx_ref[...] = jnp.where(lane == 0, i1, jnp.where(lane == 1, i2, 0))
        p1_ref[...] = jnp.broadcast_to(p1, p1_ref.shape)
        p2_ref[...] = jnp.broadcast_to(p2, p2_ref.shape)
    else:
        o_ref, = rest
    o_ref[...] = out


def _merge_call(h, a, b, m, gates, w_pa, w_pb, w_pm, w_o, tm, g_ffn=None, w_router=None, n_exp=0):
    n, d = h.shape
    row = lambda width: pl.BlockSpec((tm, width), lambda t: (t, 0))
    const = lambda w: pl.BlockSpec(w.shape, lambda t: (0, 0))
    args = [h, a, b, m, gates, w_pa, w_pb, w_pm, w_o]
    in_specs = [row(d), row(GMLP_WIDTH), row(FOX_WIDTH), row(MEM_WIDTH), row(N_BRANCH * d),
                const(w_pa), const(w_pb), const(w_pm), const(w_o)]
    out_specs = [row(d)]
    out_shape = [jax.ShapeDtypeStruct((n, d), F32)]
    if n_exp:
        args += [g_ffn, w_router]
        in_specs += [const(g_ffn), const(w_router)]
        out_specs += [row(LANES)] * 3
        out_shape += [jax.ShapeDtypeStruct((n, LANES), jnp.int32),
                      jax.ShapeDtypeStruct((n, LANES), F32), jax.ShapeDtypeStruct((n, LANES), F32)]
    return pl.pallas_call(
        functools.partial(_merge_kernel, n_exp=n_exp),
        grid=(n // tm,),
        in_specs=in_specs,
        out_specs=out_specs,
        out_shape=out_shape,
        compiler_params=_cparams("parallel"),
        name="merge_branches",
    )(*args)


def _top2(logits, n_exp):
    lane = lax.broadcasted_iota(jnp.int32, logits.shape, 1)
    neg = jnp.float32(-jnp.inf)
    logits = jnp.where(lane < n_exp, logits, neg)
    v1 = jnp.max(logits, axis=-1, keepdims=True)
    i1 = jnp.min(jnp.where(logits == v1, lane, LANES), axis=-1, keepdims=True)
    rest = jnp.where(lane == i1, neg, logits)
    v2 = jnp.max(rest, axis=-1, keepdims=True)
    i2 = jnp.min(jnp.where(rest == v2, lane, LANES), axis=-1, keepdims=True)
    e2 = jnp.exp(v2 - v1)
    return lane, i1, i2, 1.0 / (1.0 + e2), e2 / (1.0 + e2)


def _top2_combine(logits, n_exp):
    lane, i1, i2, p1, p2 = _top2(logits, n_exp)
    return jnp.where(lane == i1, p1, 0.0) + jnp.where(lane == i2, p2, 0.0)


def _mixer_kernel(h_ref, g_ref, *refs, routed, final_norm):
    refs = list(refs)
    wr_ref = refs.pop(0) if routed else None
    gfin_ref = refs.pop(0) if final_norm else None
    w1_ref, w3_ref, w2_ref, o_ref, xn_sc, acc_sc = refs[:6]
    comb_sc = refs[6] if routed else None
    e = pl.program_id(1)
    c = pl.program_id(2)
    first = jnp.logical_and(e == 0, c == 0)
    last = jnp.logical_and(e == pl.num_programs(1) - 1, c == pl.num_programs(2) - 1)

    @pl.when(first)
    def _():
        xn = _rms(h_ref[...], g_ref[...])
        xn_sc[...] = xn.astype(BF16)
        acc_sc[...] = jnp.zeros_like(acc_sc)
        if routed:
            logits = _dot(xn.astype(BF16), wr_ref[...])
            comb_sc[...] = _top2_combine(logits, pl.num_programs(1))

    xb = xn_sc[...]
    g1 = _dot(xb, w1_ref[0])
    g3 = _dot(xb, w3_ref[0])
    mid = (g1 * _sigmoid(g1) * g3).astype(BF16)
    y = _dot(mid, w2_ref[0])
    if routed:
        lane = lax.broadcasted_iota(jnp.int32, comb_sc.shape, 1)
        y = y * jnp.sum(jnp.where(lane == e, comb_sc[...], 0.0), axis=-1, keepdims=True)
    acc_sc[...] += y

    @pl.when(last)
    def _():
        out = h_ref[...] + acc_sc[...]
        if final_norm:
            out = _rms(out, gfin_ref[...])
        o_ref[...] = out


def _mixer_call(h, g, w1, w3, w2, tm, tf, w_router=None, g_final=None):
    n, d = h.shape
    n_exp, _, ff = w1.shape
    routed = w_router is not None
    final_norm = g_final is not None
    row = pl.BlockSpec((tm, d), lambda t, e, c: (t, 0))
    vec = pl.BlockSpec((1, d), lambda t, e, c: (0, 0))
    args, in_specs = [h, g], [row, vec]
    if routed:
        args.append(w_router)
        in_specs.append(pl.BlockSpec(w_router.shape, lambda t, e, c: (0, 0)))
    if final_norm:
        args.append(g_final)
        in_specs.append(vec)
    args += [w1, w3, w2]
    in_specs += [
        pl.BlockSpec((1, d, tf), lambda t, e, c: (e, 0, c)),
        pl.BlockSpec((1, d, tf), lambda t, e, c: (e, 0, c)),
        pl.BlockSpec((1, tf, d), lambda t, e, c: (e, c, 0)),
    ]
    scratch = [pltpu.VMEM((tm, d), BF16), pltpu.VMEM((tm, d), F32)]
    if routed:
        scratch.append(pltpu.VMEM((tm, LANES), F32))
    return pl.pallas_call(
        functools.partial(_mixer_kernel, routed=routed, final_norm=final_norm),
        grid=(n // tm, n_exp, ff // tf),
        in_specs=in_specs,
        out_specs=row,
        out_shape=jax.ShapeDtypeStruct((n, d), F32),
        scratch_shapes=scratch,
        compiler_params=_cparams("parallel", "arbitrary", "arbitrary"),
        name="channel_mixer",
    )(*args)


def _row_copy(src_ref, src_row, dst_ref, dst_row, sem):
    src = src_ref.at[pl.ds(pl.multiple_of(src_row * SUBLANES, SUBLANES), SUBLANES)]
    dst = dst_ref.at[pl.ds(pl.multiple_of(dst_row * SUBLANES, SUBLANES), SUBLANES)]
    return pltpu.make_async_copy(src, dst, sem)


def _slab(s, rows):
    return (pl.ds(s, rows, stride=SUBLANES), slice(None))


def _dispatch_kernel(pos1_ref, pos2_ref, h_ref, g_ref, zero_ref, xs_ref, x_sc, sem):
    del zero_ref
    tm, d = h_ref.shape
    base = pl.program_id(0) * tm
    x = _rms(h_ref[...], g_ref[...])
    for s in range(d // LANES):
        x_sc[_slab(s, tm)] = x[:, s * LANES:(s + 1) * LANES]

    def start(i, carry):
        _row_copy(x_sc, i, xs_ref, pos1_ref[base + i], sem).start(priority=0)
        _row_copy(x_sc, i, xs_ref, pos2_ref[base + i], sem).start(priority=1)
        return carry

    def wait(i, carry):
        _row_copy(x_sc, i, xs_ref, pos1_ref[base + i], sem).wait()
        _row_copy(x_sc, i, xs_ref, pos2_ref[base + i], sem).wait()
        return carry

    lax.fori_loop(0, tm, start, 0)
    lax.fori_loop(0, tm, wait, 0)


def _dispatch_call(h, g, pos1, pos2, n_rows, tm):
    n, d = h.shape
    grid_spec = pltpu.PrefetchScalarGridSpec(
        num_scalar_prefetch=2,
        grid=(n // tm,),
        in_specs=[pl.BlockSpec((tm, d), lambda t, p1, p2: (t, 0)),
                  pl.BlockSpec((1, d), lambda t, p1, p2: (0, 0)),
                  pl.BlockSpec(memory_space=pl.ANY)],
        out_specs=pl.BlockSpec(memory_space=pl.ANY),
        scratch_shapes=[pltpu.VMEM((tm * SUBLANES, LANES), F32), pltpu.SemaphoreType.DMA(())],
    )
    assert d == SUBLANES * LANES
    return pl.pallas_call(
        _dispatch_kernel,
        grid_spec=grid_spec,
        out_shape=jax.ShapeDtypeStruct((n_rows * SUBLANES, LANES), F32),
        input_output_aliases={4: 0},
        compiler_params=_cparams("arbitrary"),
        name="moe_dispatch",
    )(pos1, pos2, h, g, jnp.zeros((n_rows * SUBLANES, LANES), F32))


def _experts_kernel(te_ref, nact_ref, x_ref, w1_ref, w3_ref, w2_ref, y_ref, xb_sc, acc_sc):
    t = pl.program_id(0)
    c = pl.program_id(1)

    tm, d = acc_sc.shape

    @pl.when(c == 0)
    def _():
        for s in range(d // LANES):
            xb_sc[:, s * LANES:(s + 1) * LANES] = x_ref[_slab(s, tm)].astype(BF16)
        acc_sc[...] = jnp.zeros_like(acc_sc)

    @pl.when(t < nact_ref[0])
    def _():
        xb = xb_sc[...]
        g1 = _dot(xb, w1_ref[0])
        g3 = _dot(xb, w3_ref[0])
        mid = (g1 * _sigmoid(g1) * g3).astype(BF16)
        acc_sc[...] += _dot(mid, w2_ref[0])

    @pl.when(c == pl.num_programs(1) - 1)
    def _():
        for s in range(d // LANES):
            y_ref[_slab(s, tm)] = acc_sc[:, s * LANES:(s + 1) * LANES]


def _experts_call(xs, tile_expert, n_active, w1, w3, w2, tm, tf):
    n_rows = xs.shape[0] // SUBLANES
    d = w1.shape[1]
    ff = w1.shape[2]
    nff = ff // tf

    def chunk(t, c, te, nact):
        return jnp.where(t < nact[0], c, nff - 1)

    grid_spec = pltpu.PrefetchScalarGridSpec(
        num_scalar_prefetch=2,
        grid=(n_rows // tm, nff),
        in_specs=[
            pl.BlockSpec((tm * SUBLANES, LANES), lambda t, c, te, nact: (t, 0)),
            pl.BlockSpec((1, d, tf), lambda t, c, te, nact: (te[t], 0, chunk(t, c, te, nact))),
            pl.BlockSpec((1, d, tf), lambda t, c, te, nact: (te[t], 0, chunk(t, c, te, nact))),
            pl.BlockSpec((1, tf, d), lambda t, c, te, nact: (te[t], chunk(t, c, te, nact), 0)),
        ],
        out_specs=pl.BlockSpec((tm * SUBLANES, LANES), lambda t, c, te, nact: (t, 0)),
        scratch_shapes=[pltpu.VMEM((tm, d), BF16), pltpu.VMEM((tm, d), F32)],
    )
    return pl.pallas_call(
        _experts_kernel,
        grid_spec=grid_spec,
        out_shape=jax.ShapeDtypeStruct((n_rows * SUBLANES, LANES), F32),
        compiler_params=_cparams("parallel", "arbitrary"),
        name="moe_experts",
    )(tile_expert, n_active, xs, w1, w3, w2)


def _combine_kernel(pos1_ref, pos2_ref, h_ref, p1_ref, p2_ref, *refs, final_norm):
    refs = list(refs)
    gfin_ref = refs.pop(0) if final_norm else None
    ys_ref, o_ref, y1_sc, y2_sc, sem = refs
    tm, d = h_ref.shape
    base = pl.program_id(0) * tm

    def start(i, carry):
        _row_copy(ys_ref, pos1_ref[base + i], y1_sc, i, sem).start(priority=0)
        _row_copy(ys_ref, pos2_ref[base + i], y2_sc, i, sem).start(priority=1)
        return carry

    def wait(i, carry):
        _row_copy(ys_ref, pos1_ref[base + i], y1_sc, i, sem).wait()
        _row_copy(ys_ref, pos2_ref[base + i], y2_sc, i, sem).wait()
        return carry

    lax.fori_loop(0, tm, start, 0)
    lax.fori_loop(0, tm, wait, 0)
    w1 = p1_ref[...]
    w2 = p2_ref[...]
    out = jnp.concatenate(
        [h_ref[:, s * LANES:(s + 1) * LANES]
         + (w1 * y1_sc[_slab(s, tm)] + w2 * y2_sc[_slab(s, tm)]) for s in range(d // LANES)], axis=1)
    if final_norm:
        out = _rms(out, gfin_ref[...])
    o_ref[...] = out


def _combine_call(h, p1, p2, ys, pos1, pos2, tm, g_final=None):
    n, d = h.shape
    final_norm = g_final is not None
    row = pl.BlockSpec((tm, d), lambda t, a, b: (t, 0))
    wide = pl.BlockSpec((tm, LANES), lambda t, a, b: (t, 0))
    args, in_specs = [h, p1, p2], [row, wide, wide]
    if final_norm:
        args.append(g_final)
        in_specs.append(pl.BlockSpec((1, d), lambda t, a, b: (0, 0)))
    args.append(ys)
    in_specs.append(pl.BlockSpec(memory_space=pl.ANY))
    grid_spec = pltpu.PrefetchScalarGridSpec(
        num_scalar_prefetch=2,
        grid=(n // tm,),
        in_specs=in_specs,
        out_specs=row,
        scratch_shapes=[pltpu.VMEM((tm * SUBLANES, LANES), F32),
                        pltpu.VMEM((tm * SUBLANES, LANES), F32),
                        pltpu.SemaphoreType.DMA(())],
    )
    return pl.pallas_call(
        functools.partial(_combine_kernel, final_norm=final_norm),
        grid_spec=grid_spec,
        out_shape=jax.ShapeDtypeStruct((n, d), F32),
        compiler_params=_cparams("arbitrary"),
        name="moe_combine",
    )(pos1, pos2, *args)


def _routing_tables(idx, n_exp, tm):
    n = idx.shape[0]
    e = idx[:, :TOP_K].reshape(-1)
    onehot = (e[:, None] == jnp.arange(n_exp, dtype=jnp.int32)[None, :]).astype(jnp.int32)
    csum = jnp.cumsum(onehot, axis=0)
    rank = jnp.take_along_axis(csum, e[:, None], axis=1)[:, 0] - 1
    counts = csum[-1]
    padded = (counts + tm - 1) // tm * tm
    seg_end = jnp.cumsum(padded)
    pos = ((seg_end - padded)[e] + rank).reshape(n, TOP_K)
    n_rows = TOP_K * n + n_exp * tm
    tile_first = jnp.arange(n_rows // tm, dtype=jnp.int32) * tm
    tile_expert = jnp.minimum(jnp.sum(tile_first[:, None] >= seg_end[None, :], axis=1), n_exp - 1)
    n_active = (seg_end[-1] // tm).reshape(1)
    return pos[:, 0], pos[:, 1], tile_expert.astype(jnp.int32), n_active.astype(jnp.int32), n_rows


def _routed_moe(h, g, routing, w1, w3, w2, tm_tok, tm_rows, tf, g_final):
    n_exp = w1.shape[0]
    idx, p1, p2 = routing
    pos1, pos2, tile_expert, n_active, n_rows = _routing_tables(idx, n_exp, tm_rows)
    xs = _dispatch_call(h, g, pos1, pos2, n_rows, tm_tok)
    ys = _experts_call(xs, tile_expert, n_active, w1, w3, w2, tm_rows, tf)
    return _combine_call(h, p1, p2, ys, pos1, pos2, tm_tok, g_final)


def _largest_tile(n, cap, quantum):
    best = quantum
    t = quantum
    while t <= min(n, cap):
        if n % t == 0:
            best = t
        t += quantum
    return best


def _spatial_operands(w_s, b_s, blk):
    reps = GMLP_BLOCK // blk
    w = w_s[:, :blk, :blk] * jnp.tril(jnp.ones((blk, blk), w_s.dtype))
    eye = jnp.eye(reps, dtype=w_s.dtype)
    w = jnp.einsum("ab,gts->gatbs", eye, w).reshape(GMLP_GROUPS, GMLP_BLOCK, GMLP_BLOCK)
    b = jnp.tile(b_s[:, :blk], (1, reps))
    ws_pairs = jnp.concatenate([w[0::2], w[1::2]], axis=2).astype(BF16)
    bs_pairs = jnp.concatenate(
        [jnp.broadcast_to(b[0::2, :, None], (GMLP_GROUPS // 2, GMLP_BLOCK, GMLP_GROUP_DIM)),
         jnp.broadcast_to(b[1::2, :, None], (GMLP_GROUPS // 2, GMLP_BLOCK, GMLP_GROUP_DIM))],
        axis=2).astype(F32)
    return ws_pairs, bs_pairs


def kernel(x_prompt, x_sample, cache_fox_k, cache_fox_v, cache_fox_logf, cache_mem_k, cache_mem_v, mem_prompt, g_mix, w_in, b_f, ln_v_g, ln_v_b, w_s, b_s, g_mem, w_mem_kv, w_pa, w_pb, w_pm, w_o, g_ffn, w1, w3, w2, w_router, e_w1, e_w3, e_w2, g_final):
    depth = w_in.shape[0]
    bp, sp, d = x_prompt.shape
    bs, ts, _ = x_sample.shape
    past = cache_fox_k.shape[2]
    n_mem = mem_prompt.shape[1]
    n_exp = e_w1.shape[1]
    assert sp % GMLP_BLOCK == 0 and GMLP_BLOCK % ts == 0 and (bs * ts) % GMLP_BLOCK == 0

    off_q = 2 * GMLP_WIDTH
    off_f = off_q + 3 * FOX_WIDTH
    off_mq = off_f + FOX_HEADS
    off_g = off_mq + MEM_WIDTH

    hp = x_prompt.reshape(bp * sp, d)
    hs = x_sample.reshape(bs * ts, d)
    mem = mem_prompt.reshape(bp * n_mem, d)
    cache_k_rows = cache_fox_k.reshape(depth * bs * past * FOX_HEADS, FOX_HEAD_DIM)
    cache_v_rows = cache_fox_v.reshape(depth * bs * past * FOX_HEADS, FOX_HEAD_DIM)

    tm_p = _largest_tile(sp, 512, GMLP_BLOCK)
    tm_s = GMLP_BLOCK
    fox_group = 2 * LANES
    fox_heads = 4
    fox_tile = _largest_tile(sp, 1024, fox_group)
    tmix_p = _largest_tile(bp * sp, 1024, LANES)
    tmix_s = bs * ts

    outs = {k: [] for k in ("fk_p", "fv_p", "fl_p", "mk_p", "mv_p", "fk_s", "fv_s", "fl_s", "gv_s")}
    for i in range(depth):
        g = g_mix[i].reshape(1, d)
        wi = w_in[i]
        w_uv = wi[:, :off_q].astype(BF16)
        w_qkv = wi[:, off_q:off_f].astype(BF16)
        w_f = jnp.zeros((d, LANES), BF16).at[:, :FOX_HEADS].set(wi[:, off_f:off_mq].astype(BF16))
        w_mq = wi[:, off_mq:off_g].astype(BF16)
        w_g = wi[:, off_g:].astype(BF16)
        bfi = jnp.zeros((1, LANES), F32).at[0, :FOX_HEADS].set(b_f[i])
        lng = ln_v_g[i].reshape(1, GMLP_WIDTH)
        lnb = ln_v_b[i].reshape(1, GMLP_WIDTH)
        wpa, wpb, wpm, wo = (w[i].astype(BF16) for w in (w_pa, w_pb, w_pm, w_o))
        gf = g_ffn[i].reshape(1, d)
        j = i // 2
        if i % 2 == 0:
            mw1, mw3, mw2 = (w[j][None].astype(BF16) for w in (w1, w3, w2))
            wr = None
        else:
            mw1, mw3, mw2 = (w[j].astype(BF16) for w in (e_w1, e_w3, e_w2))
            wr = jnp.zeros((d, LANES), BF16).at[:, :n_exp].set(w_router[j].astype(BF16))
        ff = mw1.shape[2]
        tf = _largest_tile(ff, 1408, LANES)
        gfin = g_final.reshape(1, d) if i == depth - 1 else None

        ws_pairs, bs_pairs = _spatial_operands(w_s[i], b_s[i], GMLP_BLOCK)
        mk, mv = _memkv_call(mem, g_mem[i].reshape(1, d), w_mem_kv[i].astype(BF16), n_mem)
        a, _ = _uv_call(hp, g, w_uv, lng, lnb, ws_pairs, bs_pairs, tm_p)
        q_t, k, v, kb, v_t = _qkv_call(hp, g, w_qkv, tm_p, True)
        logf, m, gates, kf = _fmg_call(hp, g, w_f, bfi, w_mq, w_g, mk, mv, bp, tm_p, BF16, True)
        b = _fox_call(q_t, kb, kf, v_t, bp, fox_tile, fox_group, fox_heads)
        if wr is None:
            hp, = _merge_call(hp, a, b, m, gates, wpa, wpb, wpm, wo, tm_p)
            hp = _mixer_call(hp, gf, mw1, mw3, mw2, tmix_p, tf, None, gfin)
        else:
            hp, *routing = _merge_call(hp, a, b, m, gates, wpa, wpb, wpm, wo, tm_p, gf, wr, n_exp)
            hp = _routed_moe(hp, gf, routing, mw1, mw3, mw2, tm_p, tmix_p, tf, gfin)
        outs["fk_p"].append(k.reshape(bp, sp, FOX_HEADS, FOX_HEAD_DIM))
        outs["fv_p"].append(v.reshape(bp, sp, FOX_HEADS, FOX_HEAD_DIM))
        outs["fl_p"].append(logf.reshape(bp, sp, FOX_HEADS))
        outs["mk_p"].append(mk.reshape(bp, n_mem, MEM_HEADS, MEM_HEAD_DIM))
        outs["mv_p"].append(mv.reshape(bp, n_mem, MEM_HEADS, MEM_HEAD_DIM))

        ws_pairs, bs_pairs = _spatial_operands(w_s[i], b_s[i], ts)
        a, vn = _uv_call(hs, g, w_uv, lng, lnb, ws_pairs, bs_pairs, tm_s)
        q, k, v = _qkv_call(hs, g, w_qkv, bs * ts, False)
        logf, m, gates = _fmg_call(hs, g, w_f, bfi, w_mq, w_g,
                                   cache_mem_k[i].reshape(bs * n_mem, MEM_WIDTH),
                                   cache_mem_v[i].reshape(bs * n_mem, MEM_WIDTH), bs, ts, F32, False)
        logf = logf.reshape(bs, ts, FOX_HEADS)
        logf_all = jnp.concatenate(
            [cache_fox_logf[i].transpose(0, 2, 1), logf.transpose(0, 2, 1),
             jnp.zeros((bs, FOX_HEADS, LANES - ts), F32)], axis=2)
        neg_f = _neg_cumsum_call(logf_all.reshape(bs * FOX_HEADS, past + LANES))
        b = _fox_sample_call(q, cache_k_rows, cache_v_rows, i, k, v, neg_f, bs, past)
        hs, = _merge_call(hs, a, b, m, gates, wpa, wpb, wpm, wo, tmix_s)
        hs = _mixer_call(hs, gf, mw1, mw3, mw2, tmix_s, tf, wr, gfin)
        outs["fk_s"].append(k.reshape(bs, ts, FOX_HEADS, FOX_HEAD_DIM))
        outs["fv_s"].append(v.reshape(bs, ts, FOX_HEADS, FOX_HEAD_DIM))
        outs["fl_s"].append(logf)
        outs["gv_s"].append(vn.reshape(bs, ts, GMLP_WIDTH))

    st = {k: jnp.stack(v) for k, v in outs.items()}
    return (hp.reshape(bp, sp, d), hs.reshape(bs, ts, d),
            st["fk_p"], st["fv_p"], st["fl_p"], st["mk_p"], st["mv_p"],
            st["fk_s"], st["fv_s"], st["fl_s"], st["gv_s"])
```

```python
import functools

import numpy as np
import jax
import jax.numpy as jnp
from jax import lax
from jax.experimental import pallas as pl
from jax.experimental.pallas import tpu as pltpu

F32 = jnp.float32
BF16 = jnp.bfloat16

EPS = 1e-6
GMLP_GROUPS = 8
GMLP_GROUP_DIM = 64
GMLP_WIDTH = GMLP_GROUPS * GMLP_GROUP_DIM
GMLP_BLOCK = 128
FOX_HEADS = 8
FOX_HEAD_DIM = 128
FOX_WIDTH = FOX_HEADS * FOX_HEAD_DIM
MEM_HEADS = 4
MEM_HEAD_DIM = 128
MEM_WIDTH = MEM_HEADS * MEM_HEAD_DIM
N_BRANCH = 3
TOP_K = 2
LANES = 128
SUBLANES = 8
BF16_ROWS = 16
LOG2E = 1.4426950408889634
VMEM_LIMIT = 56 * 1024 * 1024

_NT = (((1,), (1,)), ((), ()))


def _cparams(*sem):
    return pltpu.CompilerParams(dimension_semantics=sem, vmem_limit_bytes=VMEM_LIMIT)


def _rms(x, g):
    return x * lax.rsqrt(jnp.mean(x * x, axis=-1, keepdims=True) + EPS) * g


def _sigmoid(x):
    return 1.0 / (1.0 + jnp.exp(-x))


def _dot(a, b):
    return jnp.dot(a, b, preferred_element_type=F32)


def _dot_nt(a, b):
    return lax.dot_general(a, b, _NT, preferred_element_type=F32)


def _uv_kernel(h_ref, g_ref, w_ref, lng_ref, lnb_ref, ws_ref, bs_ref, a_ref, vn_ref):
    tm = h_ref.shape[0]
    xn = _rms(h_ref[...], g_ref[...]).astype(BF16)
    z = _dot(xn, w_ref[...])
    uv = 0.5 * z * (1.0 + lax.erf(z * np.float32(2.0 ** -0.5)))
    u = uv[:, :GMLP_WIDTH]
    v = uv[:, GMLP_WIDTH:]
    mu = jnp.mean(v, axis=-1, keepdims=True)
    vc = v - mu
    var = jnp.mean(vc * vc, axis=-1, keepdims=True)
    vn = vc * lax.rsqrt(var + EPS) * lng_ref[...] + lnb_ref[...]
    vn_ref[...] = vn
    lane = lax.broadcasted_iota(jnp.int32, (GMLP_BLOCK, LANES), 1)
    low = lane < GMLP_GROUP_DIM
    for r in range(tm // GMLP_BLOCK):
        rows = slice(r * GMLP_BLOCK, (r + 1) * GMLP_BLOCK)
        for p in range(GMLP_WIDTH // LANES):
            cols = slice(p * LANES, (p + 1) * LANES)
            vs = vn[rows, cols].astype(BF16)
            zero = jnp.zeros_like(vs)
            rhs = jnp.concatenate([jnp.where(low, vs, zero), jnp.where(low, zero, vs)], axis=0)
            mixed = _dot(ws_ref[p], rhs) + bs_ref[p]
            a_ref[rows, cols] = (u[rows, cols] * mixed).astype(a_ref.dtype)


def _uv_call(h, g, w_uv, ln_g, ln_b, ws_pairs, bs_pairs, tm):
    n, d = h.shape
    const = lambda *shape: pl.BlockSpec(shape, lambda t: (0,) * len(shape))
    return pl.pallas_call(
        _uv_kernel,
        grid=(n // tm,),
        in_specs=[
            pl.BlockSpec((tm, d), lambda t: (t, 0)),
            const(1, d),
            const(d, 2 * GMLP_WIDTH),
            const(1, GMLP_WIDTH),
            const(1, GMLP_WIDTH),
            const(*ws_pairs.shape),
            const(*bs_pairs.shape),
        ],
        out_specs=[
            pl.BlockSpec((tm, GMLP_WIDTH), lambda t: (t, 0)),
            pl.BlockSpec((tm, GMLP_WIDTH), lambda t: (t, 0)),
        ],
        out_shape=[
            jax.ShapeDtypeStruct((n, GMLP_WIDTH), BF16),
            jax.ShapeDtypeStruct((n, GMLP_WIDTH), F32),
        ],
        compiler_params=_cparams("parallel"),
        name="inproj_gmlp",
    )(h, g, w_uv, ln_g, ln_b, ws_pairs, bs_pairs)


def _qkv_kernel(h_ref, g_ref, w_ref, q_ref, k_ref, v_ref, *bf_refs, for_flash):
    xn = _rms(h_ref[...], g_ref[...]).astype(BF16)
    q = _dot(xn, w_ref[:, :FOX_WIDTH])
    k = _dot(xn, w_ref[:, FOX_WIDTH:2 * FOX_WIDTH])
    v = _dot(xn, w_ref[:, 2 * FOX_WIDTH:])
    k_ref[...] = k
    v_ref[...] = v
    if for_flash:
        kb_ref, vt_ref = bf_refs
        q_ref[...] = (q * np.float32(FOX_HEAD_DIM ** -0.5 * LOG2E)).T.astype(BF16)
        kb_ref[...] = k.astype(BF16)
        vt_ref[...] = v.T.astype(BF16)
    else:
        q_ref[...] = q.astype(BF16)


def _qkv_call(h, g, w_qkv, tm, for_flash):
    n, d = h.shape
    row = lambda width: pl.BlockSpec((tm, width), lambda t: (t, 0))
    col = pl.BlockSpec((FOX_WIDTH, tm), lambda t: (0, t))
    rows_f32 = jax.ShapeDtypeStruct((n, FOX_WIDTH), F32)
    rows_bf16 = jax.ShapeDtypeStruct((n, FOX_WIDTH), BF16)
    cols_bf16 = jax.ShapeDtypeStruct((FOX_WIDTH, n), BF16)
    if for_flash:
        out_specs = [col, row(FOX_WIDTH), row(FOX_WIDTH), row(FOX_WIDTH), col]
        out_shape = [cols_bf16, rows_f32, rows_f32, rows_bf16, cols_bf16]
    else:
        out_specs = [row(FOX_WIDTH)] * 3
        out_shape = [rows_bf16, rows_f32, rows_f32]
    return pl.pallas_call(
        functools.partial(_qkv_kernel, for_flash=for_flash),
        grid=(n // tm,),
        in_specs=[row(d), pl.BlockSpec((1, d), lambda t: (0, 0)),
                  pl.BlockSpec((d, 3 * FOX_WIDTH), lambda t: (0, 0))],
        out_specs=out_specs,
        out_shape=out_shape,
        compiler_params=_cparams("parallel"),
        name="inproj_qkv",
    )(h, g, w_qkv)


def _split3(x):
    hi = x.astype(BF16)
    r = x - hi.astype(F32)
    mid = r.astype(BF16)
    lo = (r - mid.astype(F32)).astype(BF16)
    return hi, mid, lo


def _fmg_kernel(h_ref, g_ref, wf_ref, bf_ref, wmq_ref, wg_ref, mk_ref, mv_ref, *rest,
                tiles_per_seq, bias_cols):
    if bias_cols:
        tri_ref, sel_ref, logf_ref, m_ref, gates_ref, kf_ref, carry_sc = rest
    else:
        logf_ref, m_ref, gates_ref = rest
    tm = h_ref.shape[0]
    xn = _rms(h_ref[...], g_ref[...]).astype(BF16)
    zf = _dot(xn, wf_ref[...]) + bf_ref[...]
    logf = jnp.minimum(zf, 0.0) - jnp.log1p(jnp.exp(-jnp.abs(zf)))
    logf_ref[...] = logf[:, :FOX_HEADS]
    if bias_cols:
        @pl.when(pl.program_id(0) % tiles_per_seq == 0)
        def _():
            carry_sc[...] = jnp.zeros_like(carry_sc)

        lane = lax.broadcasted_iota(jnp.int32, logf.shape, 1)
        x = jnp.where(lane < FOX_HEADS, logf, 0.0)
        r = _dot(tri_ref[...], jnp.concatenate(_split3(x), axis=1))
        f = r[:, :LANES] + r[:, LANES:2 * LANES] + r[:, 2 * LANES:] + carry_sc[...]
        carry_sc[...] = f[tm - 1:tm, :]
        parts = jnp.concatenate(_split3(f * np.float32(-LOG2E)), axis=1)
        kf_ref[...] = _dot(parts, sel_ref[...]).astype(BF16)
    mq = _dot(xn, wmq_ref[...]).astype(BF16)
    for hh in range(MEM_HEADS):
        cols = slice(hh * MEM_HEAD_DIM, (hh + 1) * MEM_HEAD_DIM)
        s = _dot_nt(mq[:, cols], mk_ref[:, cols].astype(BF16)) * np.float32(MEM_HEAD_DIM ** -0.5)
        p = jnp.exp(s - jnp.max(s, axis=-1, keepdims=True))
        p = (p / jnp.sum(p, axis=-1, keepdims=True)).astype(BF16)
        m_ref[:, cols] = _dot(p, mv_ref[:, cols].astype(BF16)).astype(m_ref.dtype)
    gates_ref[...] = _sigmoid(_dot(xn, wg_ref[...])).astype(gates_ref.dtype)


def _fmg_call(h, g, w_f, b_f, w_mq, w_g, mk, mv, batch, tm, gates_dtype, bias_cols):
    n, d = h.shape
    seq = n // batch
    tpb = seq // tm
    n_mem = mk.shape[0] // batch
    const = lambda *shape: pl.BlockSpec(shape, lambda t: (0,) * len(shape))
    row = lambda width: pl.BlockSpec((tm, width), lambda t: (t, 0))
    args = [h, g, w_f, b_f, w_mq, w_g, mk, mv]
    in_specs = [
        row(d), const(1, d), const(d, LANES), const(1, LANES), const(d, MEM_WIDTH),
        const(d, N_BRANCH * d),
        pl.BlockSpec((n_mem, MEM_WIDTH), lambda t: (t // tpb, 0)),
        pl.BlockSpec((n_mem, MEM_WIDTH), lambda t: (t // tpb, 0)),
    ]
    out_specs = [row(FOX_HEADS), row(MEM_WIDTH), row(N_BRANCH * d)]
    out_shape = [
        jax.ShapeDtypeStruct((n, FOX_HEADS), F32),
        jax.ShapeDtypeStruct((n, MEM_WIDTH), BF16),
        jax.ShapeDtypeStruct((n, N_BRANCH * d), gates_dtype),
    ]
    scratch = []
    if bias_cols:
        tri = jnp.tril(jnp.ones((tm, tm), BF16))
        head = jnp.arange(FOX_HEADS)
        sel = jnp.zeros((3 * LANES, FOX_WIDTH), BF16)
        for c in range(3):
            sel = sel.at[c * LANES + head, head * FOX_HEAD_DIM + c].set(1)
        args += [tri, sel]
        in_specs += [const(tm, tm), const(3 * LANES, FOX_WIDTH)]
        out_specs.append(row(FOX_WIDTH))
        out_shape.append(jax.ShapeDtypeStruct((n, FOX_WIDTH), BF16))
        scratch.append(pltpu.VMEM((1, LANES), F32))
    return pl.pallas_call(
        functools.partial(_fmg_kernel, tiles_per_seq=tpb, bias_cols=bias_cols),
        grid=(n // tm,),
        in_specs=in_specs,
        out_specs=out_specs,
        out_shape=out_shape,
        scratch_shapes=scratch,
        compiler_params=_cparams("arbitrary" if bias_cols else "parallel"),
        name="inproj_logf_mem_gates",
    )(*args)


def _memkv_kernel(x_ref, g_ref, w_ref, mk_ref, mv_ref):
    xn = _rms(x_ref[...], g_ref[...]).astype(BF16)
    mk_ref[...] = _dot(xn, w_ref[:, :MEM_WIDTH])
    mv_ref[...] = _dot(xn, w_ref[:, MEM_WIDTH:])


def _memkv_call(mem, g, w_kv, tm):
    n, d = mem.shape
    return pl.pallas_call(
        _memkv_kernel,
        grid=(n // tm,),
        in_specs=[pl.BlockSpec((tm, d), lambda t: (t, 0)),
                  pl.BlockSpec((1, d), lambda t: (0, 0)),
                  pl.BlockSpec((d, 2 * MEM_WIDTH), lambda t: (0, 0))],
        out_specs=[pl.BlockSpec((tm, MEM_WIDTH), lambda t: (t, 0))] * 2,
        out_shape=[jax.ShapeDtypeStruct((n, MEM_WIDTH), F32)] * 2,
        compiler_params=_cparams("parallel"),
        name="memory_kv",
    )(mem, g, w_kv)


def _neg_cumsum_kernel(x_ref, o_ref):
    x = x_ref[...]
    width = x.shape[-1]
    lane = lax.broadcasted_iota(jnp.int32, x.shape, 1)
    d = 1
    while d < width:
        x = x + jnp.where(lane >= d, pltpu.roll(x, d, 1), 0.0)
        d *= 2
    o_ref[...] = -x


def _neg_cumsum_call(x):
    return pl.pallas_call(
        _neg_cumsum_kernel,
        out_shape=jax.ShapeDtypeStruct(x.shape, F32),
        compiler_params=pltpu.CompilerParams(vmem_limit_bytes=VMEM_LIMIT),
        name="logf_cumsum",
    )(x)


def _fox_kernel(qi_ref, kj_ref, flag_ref, qt_ref, ones_ref, k_ref, kf_ref, vt_ref, o_ref,
                m_sc, acc_sc, *, tile, group, heads):
    step = pl.program_id(2)
    flags = flag_ref[step]
    acc_rows = FOX_HEAD_DIM + BF16_ROWS

    @pl.when(flags & 1 != 0)
    def _():
        m_sc[...] = jnp.full_like(m_sc, -jnp.inf)
        acc_sc[...] = jnp.zeros_like(acc_sc)

    def update(diagonal):
        n_groups = tile // group
        scores = []
        for hd in range(heads):
            hs = slice(hd * FOX_HEAD_DIM, (hd + 1) * FOX_HEAD_DIM)
            ka = jnp.concatenate([k_ref[:, hs], kf_ref[:, hs]], axis=1)
            for gi in range(n_groups):
                cols = slice(gi * group, (gi + 1) * group)
                rows = (gi + 1) * group if diagonal else tile
                qa = jnp.concatenate([qt_ref[hs, cols], ones_ref[:, :group]], axis=0)
                s = _dot(ka[:rows], qa)
                if diagonal:
                    key = lax.broadcasted_iota(jnp.int32, s.shape, 0)
                    qry = lax.broadcasted_iota(jnp.int32, s.shape, 1) + gi * group
                    s = jnp.where(key <= qry, s, -jnp.inf)
                scores.append(s)
        for hd in range(heads):
            hs = slice(hd * FOX_HEAD_DIM, (hd + 1) * FOX_HEAD_DIM)
            ar = slice(hd * acc_rows, (hd + 1) * acc_rows)
            va = jnp.concatenate([vt_ref[hs, :], ones_ref[:BF16_ROWS]], axis=0)
            for gi in range(n_groups):
                s = scores[hd * n_groups + gi]
                cols = slice(gi * group, (gi + 1) * group)
                rows = s.shape[0]
                m_prev = m_sc[hd:hd + 1, cols]
                m_new = jnp.maximum(m_prev, jnp.max(s, axis=0, keepdims=True))
                p = jnp.exp2(s - m_new).astype(BF16)
                acc_sc[ar, cols] = (jnp.exp2(m_prev - m_new) * acc_sc[ar, cols]
                                    + _dot(va[:, :rows], p))
                m_sc[hd:hd + 1, cols] = m_new

    @pl.when(flags & 2 == 0)
    def _():
        update(False)

    @pl.when(flags & 2 != 0)
    def _():
        update(True)
        for hd in range(heads):
            acc = acc_sc[hd * acc_rows:(hd + 1) * acc_rows]
            out = acc[:FOX_HEAD_DIM] * (1.0 / acc[FOX_HEAD_DIM:FOX_HEAD_DIM + 1])
            o_ref[:, hd * FOX_HEAD_DIM:(hd + 1) * FOX_HEAD_DIM] = out.T.astype(o_ref.dtype)


def _fox_tables(n_tiles):
    qi, kj, flags = [], [], []
    for i in range(n_tiles):
        for j in range(i + 1):
            qi.append(i)
            kj.append(j)
            flags.append((1 if j == 0 else 0) | (2 if j == i else 0))
    return (np.asarray(qi, np.int32), np.asarray(kj, np.int32), np.asarray(flags, np.int32))


def _fox_call(q_t, k, kf, v_t, batch, tile, group, heads_per_step):
    width, n = q_t.shape
    seq = n // batch
    hw = heads_per_step * FOX_HEAD_DIM
    nt = seq // tile
    qi, kj, flags = _fox_tables(nt)
    ones = jnp.zeros((FOX_HEAD_DIM, tile), BF16).at[:3].set(1)
    q_map = lambda b, h, s, qi, kj, fl: (h, b * nt + qi[s])
    kt_map = lambda b, h, s, qi, kj, fl: (h, b * nt + kj[s])
    k_map = lambda b, h, s, qi, kj, fl: (b * nt + kj[s], h)
    grid_spec = pltpu.PrefetchScalarGridSpec(
        num_scalar_prefetch=3,
        grid=(batch, width // hw, len(qi)),
        in_specs=[
            pl.BlockSpec((hw, tile), q_map),
            pl.BlockSpec((FOX_HEAD_DIM, tile), lambda b, h, s, qi, kj, fl: (0, 0)),
            pl.BlockSpec((tile, hw), k_map),
            pl.BlockSpec((tile, hw), k_map),
            pl.BlockSpec((hw, tile), kt_map),
        ],
        out_specs=pl.BlockSpec((tile, hw), lambda b, h, s, qi, kj, fl: (b * nt + qi[s], h)),
        scratch_shapes=[
            pltpu.VMEM((heads_per_step, tile), F32),
            pltpu.VMEM((heads_per_step * (FOX_HEAD_DIM + BF16_ROWS), tile), F32),
        ],
    )
    return pl.pallas_call(
        functools.partial(_fox_kernel, tile=tile, group=group, heads=heads_per_step),
        grid_spec=grid_spec,
        out_shape=jax.ShapeDtypeStruct((n, width), BF16),
        compiler_params=_cparams("parallel", "parallel", "arbitrary"),
        name="fox_attention",
    )(jnp.asarray(qi), jnp.asarray(kj), jnp.asarray(flags), q_t, ones, k, kf, v_t)


def _fox_sample_kernel(q_ref, kc_ref, vc_ref, kn_ref, vn_ref, nf_ref, o_ref, *, heads):
    t = q_ref.shape[0]
    past = kc_ref.shape[0] // heads
    scale = np.float32(FOX_HEAD_DIM ** -0.5)
    pad = jnp.zeros((LANES - t, FOX_HEAD_DIM), BF16)
    row = lax.broadcasted_iota(jnp.int32, (t, LANES), 0)
    col = lax.broadcasted_iota(jnp.int32, (t, LANES), 1)
    for hd in range(heads):
        hs = slice(hd * FOX_HEAD_DIM, (hd + 1) * FOX_HEAD_DIM)
        cached = (pl.ds(hd, past, stride=heads), slice(None))
        q = q_ref[:, hs]
        nf = nf_ref[hd]
        kn = jnp.concatenate([kn_ref[:, hs].astype(BF16), pad], axis=0)
        vn = jnp.concatenate([vn_ref[:, hs].astype(BF16), pad], axis=0)
        s_c = _dot_nt(q, kc_ref[cached].astype(BF16)) * scale + nf[:, :past]
        s_n = _dot_nt(q, kn) * scale + nf[:, past:]
        s_n = jnp.where(col <= row, s_n, -jnp.inf)
        m = jnp.maximum(jnp.max(s_c, axis=-1, keepdims=True), jnp.max(s_n, axis=-1, keepdims=True))
        p_c = jnp.exp(s_c - m)
        p_n = jnp.exp(s_n - m)
        l = jnp.sum(p_c, axis=-1, keepdims=True) + jnp.sum(p_n, axis=-1, keepdims=True)
        p_c = (p_c / l).astype(BF16)
        p_n = (p_n / l).astype(BF16)
        o_ref[:, hs] = (_dot(p_c, vc_ref[cached].astype(BF16)) + _dot(p_n, vn)).astype(o_ref.dtype)


def _fox_sample_call(q, k_cache, v_cache, layer, k_new, v_new, neg_f, batch, past):
    n, width = q.shape
    t = n // batch
    heads = width // FOX_HEAD_DIM
    nf = neg_f.reshape(batch * heads, 1, past + LANES)
    new = pl.BlockSpec((t, width), lambda b: (b, 0))
    old = pl.BlockSpec((past * heads, FOX_HEAD_DIM), lambda b: (layer * batch + b, 0))
    return pl.pallas_call(
        functools.partial(_fox_sample_kernel, heads=heads),
        grid=(batch,),
        in_specs=[new, old, old, new, new,
                  pl.BlockSpec((heads, 1, past + LANES), lambda b: (b, 0, 0))],
        out_specs=new,
        out_shape=jax.ShapeDtypeStruct((n, width), BF16),
        compiler_params=_cparams("parallel"),
        name="fox_attention_sample",
    )(q, k_cache, v_cache, k_new, v_new, nf)


def _merge_kernel(h_ref, a_ref, b_ref, m_ref, gates_ref, wpa_ref, wpb_ref, wpm_ref, wo_ref, *rest,
                  n_exp):
    d = h_ref.shape[1]
    y = gates_ref[:, :d].astype(F32) * _dot(a_ref[...], wpa_ref[...])
    y = y + gates_ref[:, d:2 * d].astype(F32) * _dot(b_ref[...], wpb_ref[...])
    y = y + gates_ref[:, 2 * d:].astype(F32) * _dot(m_ref[...], wpm_ref[...])
    out = h_ref[...] + _dot(y.astype(BF16), wo_ref[...])
    if n_exp:
        gf_ref, wr_ref, o_ref, idx_ref, p1_ref, p2_ref = rest
        xn = _rms(out, gf_ref[...]).astype(BF16)
        lane, i1, i2, p1, p2 = _top2(_dot(xn, wr_ref[...]), n_exp)
        idx_ref[...] = jnp.where(lane == 0, i1, jnp.where(lane == 1, i2, 0))
        p1_ref[...] = jnp.broadcast_to(p1, p1_ref.shape)
        p2_ref[...] = jnp.broadcast_to(p2, p2_ref.shape)
    else:
        o_ref, = rest
    o_ref[...] = out


def _merge_call(h, a, b, m, gates, w_pa, w_pb, w_pm, w_o, tm, g_ffn=None, w_router=None, n_exp=0):
    n, d = h.shape
    row = lambda width: pl.BlockSpec((tm, width), lambda t: (t, 0))
    const = lambda w: pl.BlockSpec(w.shape, lambda t: (0, 0))
    args = [h, a, b, m, gates, w_pa, w_pb, w_pm, w_o]
    in_specs = [row(d), row(GMLP_WIDTH), row(FOX_WIDTH), row(MEM_WIDTH), row(N_BRANCH * d),
                const(w_pa), const(w_pb), const(w_pm), const(w_o)]
    out_specs = [row(d)]
    out_shape = [jax.ShapeDtypeStruct((n, d), F32)]
    if n_exp:
        args += [g_ffn, w_router]
        in_specs += [const(g_ffn), const(w_router)]
        out_specs += [row(LANES)] * 3
        out_shape += [jax.ShapeDtypeStruct((n, LANES), jnp.int32),
                      jax.ShapeDtypeStruct((n, LANES), F32), jax.ShapeDtypeStruct((n, LANES), F32)]
    return pl.pallas_call(
        functools.partial(_merge_kernel, n_exp=n_exp),
        grid=(n // tm,),
        in_specs=in_specs,
        out_specs=out_specs,
        out_shape=out_shape,
        compiler_params=_cparams("parallel"),
        name="merge_branches",
    )(*args)


def _top2(logits, n_exp):
    lane = lax.broadcasted_iota(jnp.int32, logits.shape, 1)
    neg = jnp.float32(-jnp.inf)
    logits = jnp.where(lane < n_exp, logits, neg)
    v1 = jnp.max(logits, axis=-1, keepdims=True)
    i1 = jnp.min(jnp.where(logits == v1, lane, LANES), axis=-1, keepdims=True)
    rest = jnp.where(lane == i1, neg, logits)
    v2 = jnp.max(rest, axis=-1, keepdims=True)
    i2 = jnp.min(jnp.where(rest == v2, lane, LANES), axis=-1, keepdims=True)
    e2 = jnp.exp(v2 - v1)
    return lane, i1, i2, 1.0 / (1.0 + e2), e2 / (1.0 + e2)


def _top2_combine(logits, n_exp):
    lane, i1, i2, p1, p2 = _top2(logits, n_exp)
    return jnp.where(lane == i1, p1, 0.0) + jnp.where(lane == i2, p2, 0.0)


def _mixer_kernel(h_ref, g_ref, *refs, routed, final_norm):
    refs = list(refs)
    wr_ref = refs.pop(0) if routed else None
    gfin_ref = refs.pop(0) if final_norm else None
    w1_ref, w3_ref, w2_ref, o_ref, xn_sc, acc_sc = refs[:6]
    comb_sc = refs[6] if routed else None
    e = pl.program_id(1)
    c = pl.program_id(2)
    first = jnp.logical_and(e == 0, c == 0)
    last = jnp.logical_and(e == pl.num_programs(1) - 1, c == pl.num_programs(2) - 1)

    @pl.when(first)
    def _():
        xn = _rms(h_ref[...], g_ref[...])
        xn_sc[...] = xn.astype(BF16)
        acc_sc[...] = jnp.zeros_like(acc_sc)
        if routed:
            logits = _dot(xn.astype(BF16), wr_ref[...])
            comb_sc[...] = _top2_combine(logits, pl.num_programs(1))

    xb = xn_sc[...]
    g1 = _dot(xb, w1_ref[0])
    g3 = _dot(xb, w3_ref[0])
    mid = (g1 * _sigmoid(g1) * g3).astype(BF16)
    y = _dot(mid, w2_ref[0])
    if routed:
        lane = lax.broadcasted_iota(jnp.int32, comb_sc.shape, 1)
        y = y * jnp.sum(jnp.where(lane == e, comb_sc[...], 0.0), axis=-1, keepdims=True)
    acc_sc[...] += y

    @pl.when(last)
    def _():
        out = h_ref[...] + acc_sc[...]
        if final_norm:
            out = _rms(out, gfin_ref[...])
        o_ref[...] = out


def _mixer_call(h, g, w1, w3, w2, tm, tf, w_router=None, g_final=None):
    n, d = h.shape
    n_exp, _, ff = w1.shape
    routed = w_router is not None
    final_norm = g_final is not None
    row = pl.BlockSpec((tm, d), lambda t, e, c: (t, 0))
    vec = pl.BlockSpec((1, d), lambda t, e, c: (0, 0))
    args, in_specs = [h, g], [row, vec]
    if routed:
        args.append(w_router)
        in_specs.append(pl.BlockSpec(w_router.shape, lambda t, e, c: (0, 0)))
    if final_norm:
        args.append(g_final)
        in_specs.append(vec)
    args += [w1, w3, w2]
    in_specs += [
        pl.BlockSpec((1, d, tf), lambda t, e, c: (e, 0, c)),
        pl.BlockSpec((1, d, tf), lambda t, e, c: (e, 0, c)),
        pl.BlockSpec((1, tf, d), lambda t, e, c: (e, c, 0)),
    ]
    scratch = [pltpu.VMEM((tm, d), BF16), pltpu.VMEM((tm, d), F32)]
    if routed:
        scratch.append(pltpu.VMEM((tm, LANES), F32))
    return pl.pallas_call(
        functools.partial(_mixer_kernel, routed=routed, final_norm=final_norm),
        grid=(n // tm, n_exp, ff // tf),
        in_specs=in_specs,
        out_specs=row,
        out_shape=jax.ShapeDtypeStruct((n, d), F32),
        scratch_shapes=scratch,
        compiler_params=_cparams("parallel", "arbitrary", "arbitrary"),
        name="channel_mixer",
    )(*args)


def _row_copy(src_ref, src_row, dst_ref, dst_row, sem):
    src = src_ref.at[pl.ds(pl.multiple_of(src_row * SUBLANES, SUBLANES), SUBLANES)]
    dst = dst_ref.at[pl.ds(pl.multiple_of(dst_row * SUBLANES, SUBLANES), SUBLANES)]
    return pltpu.make_async_copy(src, dst, sem)


def _slab(s, rows):
    return (pl.ds(s, rows, stride=SUBLANES), slice(None))


def _dispatch_kernel(pos1_ref, pos2_ref, h_ref, g_ref, zero_ref, xs_ref, x_sc, sem):
    del zero_ref
    tm, d = h_ref.shape
    base = pl.program_id(0) * tm
    x = _rms(h_ref[...], g_ref[...])
    for s in range(d // LANES):
        x_sc[_slab(s, tm)] = x[:, s * LANES:(s + 1) * LANES]

    def start(i, carry):
        _row_copy(x_sc, i, xs_ref, pos1_ref[base + i], sem).start(priority=0)
        _row_copy(x_sc, i, xs_ref, pos2_ref[base + i], sem).start(priority=1)
        return carry

    def wait(i, carry):
        _row_copy(x_sc, i, xs_ref, pos1_ref[base + i], sem).wait()
        _row_copy(x_sc, i, xs_ref, pos2_ref[base + i], sem).wait()
        return carry

    lax.fori_loop(0, tm, start, 0)
    lax.fori_loop(0, tm, wait, 0)


def _dispatch_call(h, g, pos1, pos2, n_rows, tm):
    n, d = h.shape
    grid_spec = pltpu.PrefetchScalarGridSpec(
        num_scalar_prefetch=2,
        grid=(n // tm,),
        in_specs=[pl.BlockSpec((tm, d), lambda t, p1, p2: (t, 0)),
                  pl.BlockSpec((1, d), lambda t, p1, p2: (0, 0)),
                  pl.BlockSpec(memory_space=pl.ANY)],
        out_specs=pl.BlockSpec(memory_space=pl.ANY),
        scratch_shapes=[pltpu.VMEM((tm * SUBLANES, LANES), F32), pltpu.SemaphoreType.DMA(())],
    )
    assert d == SUBLANES * LANES
    return pl.pallas_call(
        _dispatch_kernel,
        grid_spec=grid_spec,
        out_shape=jax.ShapeDtypeStruct((n_rows * SUBLANES, LANES), F32),
        input_output_aliases={4: 0},
        compiler_params=_cparams("arbitrary"),
        name="moe_dispatch",
    )(pos1, pos2, h, g, jnp.zeros((n_rows * SUBLANES, LANES), F32))


def _experts_kernel(te_ref, nact_ref, x_ref, w1_ref, w3_ref, w2_ref, y_ref, xb_sc, acc_sc):
    t = pl.program_id(0)
    c = pl.program_id(1)

    tm, d = acc_sc.shape

    @pl.when(c == 0)
    def _():
        for s in range(d // LANES):
            xb_sc[:, s * LANES:(s + 1) * LANES] = x_ref[_slab(s, tm)].astype(BF16)
        acc_sc[...] = jnp.zeros_like(acc_sc)

    @pl.when(t < nact_ref[0])
    def _():
        xb = xb_sc[...]
        g1 = _dot(xb, w1_ref[0])
        g3 = _dot(xb, w3_ref[0])
        mid = (g1 * _sigmoid(g1) * g3).astype(BF16)
        acc_sc[...] += _dot(mid, w2_ref[0])

    @pl.when(c == pl.num_programs(1) - 1)
    def _():
        for s in range(d // LANES):
            y_ref[_slab(s, tm)] = acc_sc[:, s * LANES:(s + 1) * LANES]


def _experts_call(xs, tile_expert, n_active, w1, w3, w2, tm, tf):
    n_rows = xs.shape[0] // SUBLANES
    d = w1.shape[1]
    ff = w1.shape[2]
    nff = ff // tf

    def chunk(t, c, te, nact):
        return jnp.where(t < nact[0], c, nff - 1)

    grid_spec = pltpu.PrefetchScalarGridSpec(
        num_scalar_prefetch=2,
        grid=(n_rows // tm, nff),
        in_specs=[
            pl.BlockSpec((tm * SUBLANES, LANES), lambda t, c, te, nact: (t, 0)),
            pl.BlockSpec((1, d, tf), lambda t, c, te, nact: (te[t], 0, chunk(t, c, te, nact))),
            pl.BlockSpec((1, d, tf), lambda t, c, te, nact: (te[t], 0, chunk(t, c, te, nact))),
            pl.BlockSpec((1, tf, d), lambda t, c, te, nact: (te[t], chunk(t, c, te, nact), 0)),
        ],
        out_specs=pl.BlockSpec((tm * SUBLANES, LANES), lambda t, c, te, nact: (t, 0)),
        scratch_shapes=[pltpu.VMEM((tm, d), BF16), pltpu.VMEM((tm, d), F32)],
    )
    return pl.pallas_call(
        _experts_kernel,
        grid_spec=grid_spec,
        out_shape=jax.ShapeDtypeStruct((n_rows * SUBLANES, LANES), F32),
        compiler_params=_cparams("parallel", "arbitrary"),
        name="moe_experts",
    )(tile_expert, n_active, xs, w1, w3, w2)


def _combine_kernel(pos1_ref, pos2_ref, h_ref, p1_ref, p2_ref, *refs, final_norm):
    refs = list(refs)
    gfin_ref = refs.pop(0) if final_norm else None
    ys_ref, o_ref, y1_sc, y2_sc, sem = refs
    tm, d = h_ref.shape
    base = pl.program_id(0) * tm

    def start(i, carry):
        _row_copy(ys_ref, pos1_ref[base + i], y1_sc, i, sem).start(priority=0)
        _row_copy(ys_ref, pos2_ref[base + i], y2_sc, i, sem).start(priority=1)
        return carry

    def wait(i, carry):
        _row_copy(ys_ref, pos1_ref[base + i], y1_sc, i, sem).wait()
        _row_copy(ys_ref, pos2_ref[base + i], y2_sc, i, sem).wait()
        return carry

    lax.fori_loop(0, tm, start, 0)
    lax.fori_loop(0, tm, wait, 0)
    w1 = p1_ref[...]
    w2 = p2_ref[...]
    out = jnp.concatenate(
        [h_ref[:, s * LANES:(s + 1) * LANES]
         + (w1 * y1_sc[_slab(s, tm)] + w2 * y2_sc[_slab(s, tm)]) for s in range(d // LANES)], axis=1)
    if final_norm:
        out = _rms(out, gfin_ref[...])
    o_ref[...] = out


def _combine_call(h, p1, p2, ys, pos1, pos2, tm, g_final=None):
    n, d = h.shape
    final_norm = g_final is not None
    row = pl.BlockSpec((tm, d), lambda t, a, b: (t, 0))
    wide = pl.BlockSpec((tm, LANES), lambda t, a, b: (t, 0))
    args, in_specs = [h, p1, p2], [row, wide, wide]
    if final_norm:
        args.append(g_final)
        in_specs.append(pl.BlockSpec((1, d), lambda t, a, b: (0, 0)))
    args.append(ys)
    in_specs.append(pl.BlockSpec(memory_space=pl.ANY))
    grid_spec = pltpu.PrefetchScalarGridSpec(
        num_scalar_prefetch=2,
        grid=(n // tm,),
        in_specs=in_specs,
        out_specs=row,
        scratch_shapes=[pltpu.VMEM((tm * SUBLANES, LANES), F32),
                        pltpu.VMEM((tm * SUBLANES, LANES), F32),
                        pltpu.SemaphoreType.DMA(())],
    )
    return pl.pallas_call(
        functools.partial(_combine_kernel, final_norm=final_norm),
        grid_spec=grid_spec,
        out_shape=jax.ShapeDtypeStruct((n, d), F32),
        compiler_params=_cparams("arbitrary"),
        name="moe_combine",
    )(pos1, pos2, *args)


def _routing_tables(idx, n_exp, tm):
    n = idx.shape[0]
    e = idx[:, :TOP_K].reshape(-1)
    onehot = (e[:, None] == jnp.arange(n_exp, dtype=jnp.int32)[None, :]).astype(jnp.int32)
    csum = jnp.cumsum(onehot, axis=0)
    rank = jnp.take_along_axis(csum, e[:, None], axis=1)[:, 0] - 1
    counts = csum[-1]
    padded = (counts + tm - 1) // tm * tm
    seg_end = jnp.cumsum(padded)
    pos = ((seg_end - padded)[e] + rank).reshape(n, TOP_K)
    n_rows = TOP_K * n + n_exp * tm
    tile_first = jnp.arange(n_rows // tm, dtype=jnp.int32) * tm
    tile_expert = jnp.minimum(jnp.sum(tile_first[:, None] >= seg_end[None, :], axis=1), n_exp - 1)
    n_active = (seg_end[-1] // tm).reshape(1)
    return pos[:, 0], pos[:, 1], tile_expert.astype(jnp.int32), n_active.astype(jnp.int32), n_rows


def _routed_moe(h, g, routing, w1, w3, w2, tm_tok, tm_rows, tf, g_final):
    n_exp = w1.shape[0]
    idx, p1, p2 = routing
    pos1, pos2, tile_expert, n_active, n_rows = _routing_tables(idx, n_exp, tm_rows)
    xs = _dispatch_call(h, g, pos1, pos2, n_rows, tm_tok)
    ys = _experts_call(xs, tile_expert, n_active, w1, w3, w2, tm_rows, tf)
    return _combine_call(h, p1, p2, ys, pos1, pos2, tm_tok, g_final)


def _largest_tile(n, cap, quantum):
    best = quantum
    t = quantum
    while t <= min(n, cap):
        if n % t == 0:
            best = t
        t += quantum
    return best


def _spatial_operands(w_s, b_s, blk):
    reps = GMLP_BLOCK // blk
    w = w_s[:, :blk, :blk] * jnp.tril(jnp.ones((blk, blk), w_s.dtype))
    eye = jnp.eye(reps, dtype=w_s.dtype)
    w = jnp.einsum("ab,gts->gatbs", eye, w).reshape(GMLP_GROUPS, GMLP_BLOCK, GMLP_BLOCK)
    b = jnp.tile(b_s[:, :blk], (1, reps))
    ws_pairs = jnp.concatenate([w[0::2], w[1::2]], axis=2).astype(BF16)
    bs_pairs = jnp.concatenate(
        [jnp.broadcast_to(b[0::2, :, None], (GMLP_GROUPS // 2, GMLP_BLOCK, GMLP_GROUP_DIM)),
         jnp.broadcast_to(b[1::2, :, None], (GMLP_GROUPS // 2, GMLP_BLOCK, GMLP_GROUP_DIM))],
        axis=2).astype(F32)
    return ws_pairs, bs_pairs


def kernel(x_prompt, x_sample, cache_fox_k, cache_fox_v, cache_fox_logf, cache_mem_k, cache_mem_v, mem_prompt, g_mix, w_in, b_f, ln_v_g, ln_v_b, w_s, b_s, g_mem, w_mem_kv, w_pa, w_pb, w_pm, w_o, g_ffn, w1, w3, w2, w_router, e_w1, e_w3, e_w2, g_final):
    depth = w_in.shape[0]
    bp, sp, d = x_prompt.shape
    bs, ts, _ = x_sample.shape
    past = cache_fox_k.shape[2]
    n_mem = mem_prompt.shape[1]
    n_exp = e_w1.shape[1]
    assert sp % GMLP_BLOCK == 0 and GMLP_BLOCK % ts == 0 and (bs * ts) % GMLP_BLOCK == 0

    off_q = 2 * GMLP_WIDTH
    off_f = off_q + 3 * FOX_WIDTH
    off_mq = off_f + FOX_HEADS
    off_g = off_mq + MEM_WIDTH

    hp = x_prompt.reshape(bp * sp, d)
    hs = x_sample.reshape(bs * ts, d)
    mem = mem_prompt.reshape(bp * n_mem, d)
    cache_k_rows = cache_fox_k.reshape(depth * bs * past * FOX_HEADS, FOX_HEAD_DIM)
    cache_v_rows = cache_fox_v.reshape(depth * bs * past * FOX_HEADS, FOX_HEAD_DIM)

    tm_p = _largest_tile(sp, 512, GMLP_BLOCK)
    tm_s = GMLP_BLOCK
    fox_group = 2 * LANES
    fox_heads = 4
    fox_tile = _largest_tile(sp, 1024, fox_group)
    tmix_p = _largest_tile(bp * sp, 1024, LANES)
    tmix_s = bs * ts

    outs = {k: [] for k in ("fk_p", "fv_p", "fl_p", "mk_p", "mv_p", "fk_s", "fv_s", "fl_s", "gv_s")}
    for i in range(depth):
        g = g_mix[i].reshape(1, d)
        wi = w_in[i]
        w_uv = wi[:, :off_q].astype(BF16)
        w_qkv = wi[:, off_q:off_f].astype(BF16)
        w_f = jnp.zeros((d, LANES), BF16).at[:, :FOX_HEADS].set(wi[:, off_f:off_mq].astype(BF16))
        w_mq = wi[:, off_mq:off_g].astype(BF16)
        w_g = wi[:, off_g:].astype(BF16)
        bfi = jnp.zeros((1, LANES), F32).at[0, :FOX_HEADS].set(b_f[i])
        lng = ln_v_g[i].reshape(1, GMLP_WIDTH)
        lnb = ln_v_b[i].reshape(1, GMLP_WIDTH)
        wpa, wpb, wpm, wo = (w[i].astype(BF16) for w in (w_pa, w_pb, w_pm, w_o))
        gf = g_ffn[i].reshape(1, d)
        j = i // 2
        if i % 2 == 0:
            mw1, mw3, mw2 = (w[j][None].astype(BF16) for w in (w1, w3, w2))
            wr = None
        else:
            mw1, mw3, mw2 = (w[j].astype(BF16) for w in (e_w1, e_w3, e_w2))
            wr = jnp.zeros((d, LANES), BF16).at[:, :n_exp].set(w_router[j].astype(BF16))
        ff = mw1.shape[2]
        tf = _largest_tile(ff, 1408, LANES)
        gfin = g_final.reshape(1, d) if i == depth - 1 else None

        ws_pairs, bs_pairs = _spatial_operands(w_s[i], b_s[i], GMLP_BLOCK)
        mk, mv = _memkv_call(mem, g_mem[i].reshape(1, d), w_mem_kv[i].astype(BF16), n_mem)
        a, _ = _uv_call(hp, g, w_uv, lng, lnb, ws_pairs, bs_pairs, tm_p)
        q_t, k, v, kb, v_t = _qkv_call(hp, g, w_qkv, tm_p, True)
        logf, m, gates, kf = _fmg_call(hp, g, w_f, bfi, w_mq, w_g, mk, mv, bp, tm_p, BF16, True)
        b = _fox_call(q_t, kb, kf, v_t, bp, fox_tile, fox_group, fox_heads)
        if wr is None:
            hp, = _merge_call(hp, a, b, m, gates, wpa, wpb, wpm, wo, tm_p)
            hp = _mixer_call(hp, gf, mw1, mw3, mw2, tmix_p, tf, None, gfin)
        else:
            hp, *routing = _merge_call(hp, a, b, m, gates, wpa, wpb, wpm, wo, tm_p, gf, wr, n_exp)
            hp = _routed_moe(hp, gf, routing, mw1, mw3, mw2, tmix_p, tmix_p, tf, gfin)
        outs["fk_p"].append(k.reshape(bp, sp, FOX_HEADS, FOX_HEAD_DIM))
        outs["fv_p"].append(v.reshape(bp, sp, FOX_HEADS, FOX_HEAD_DIM))
        outs["fl_p"].append(logf.reshape(bp, sp, FOX_HEADS))
        outs["mk_p"].append(mk.reshape(bp, n_mem, MEM_HEADS, MEM_HEAD_DIM))
        outs["mv_p"].append(mv.reshape(bp, n_mem, MEM_HEADS, MEM_HEAD_DIM))

        ws_pairs, bs_pairs = _spatial_operands(w_s[i], b_s[i], ts)
        a, vn = _uv_call(hs, g, w_uv, lng, lnb, ws_pairs, bs_pairs, tm_s)
        q, k, v = _qkv_call(hs, g, w_qkv, bs * ts, False)
        logf, m, gates = _fmg_call(hs, g, w_f, bfi, w_mq, w_g,
                                   cache_mem_k[i].reshape(bs * n_mem, MEM_WIDTH),
                                   cache_mem_v[i].reshape(bs * n_mem, MEM_WIDTH), bs, ts, F32, False)
        logf = logf.reshape(bs, ts, FOX_HEADS)
        logf_all = jnp.concatenate(
            [cache_fox_logf[i].transpose(0, 2, 1), logf.transpose(0, 2, 1),
             jnp.zeros((bs, FOX_HEADS, LANES - ts), F32)], axis=2)
        neg_f = _neg_cumsum_call(logf_all.reshape(bs * FOX_HEADS, past + LANES))
        b = _fox_sample_call(q, cache_k_rows, cache_v_rows, i, k, v, neg_f, bs, past)
        hs, = _merge_call(hs, a, b, m, gates, wpa, wpb, wpm, wo, tmix_s)
        hs = _mixer_call(hs, gf, mw1, mw3, mw2, tmix_s, tf, wr, gfin)
        outs["fk_s"].append(k.reshape(bs, ts, FOX_HEADS, FOX_HEAD_DIM))
        outs["fv_s"].append(v.reshape(bs, ts, FOX_HEADS, FOX_HEAD_DIM))
        outs["fl_s"].append(logf)
        outs["gv_s"].append(vn.reshape(bs, ts, GMLP_WIDTH))

    st = {k: jnp.stack(v) for k, v in outs.items()}
    return (hp.reshape(bp, sp, d), hs.reshape(bs, ts, d),
            st["fk_p"], st["fv_p"], st["fl_p"], st["mk_p"], st["mv_p"],
            st["fk_s"], st["fv_s"], st["fl_s"], st["gv_s"])
```
